```python
import jax
import jax.numpy as jnp
from jax import lax
import numpy as np

D_MODEL = 1024
BATCH = 16
SEQ = 4096
DEPTH = 4
DEC_BATCH = 32
DEC_SEQ = 16
PAST_LEN = 4096

CHUNK = 64
A_HEADS = 8
A_KV_HEADS = 2
A_GROUP = A_HEADS // A_KV_HEADS
HEAD_DIM = 64
A_Q = A_HEADS * HEAD_DIM
A_KV = A_KV_HEADS * HEAD_DIM
WINDOW = 128
WIN_CHUNKS = WINDOW // CHUNK
ROPE_DIM = HEAD_DIM // 4
ROPE_THETA = 500000.0
CONV_CH = D_MODEL // 2
CONV_W = 31
M_HEADS = 4
M_WIDTH = D_MODEL
M_DIM = M_WIDTH // M_HEADS
QK_CONV_W = 4
D_FF = 4 * D_MODEL
EVEN_IN = A_Q + 2 * A_KV + 2 * CONV_CH
EVEN_MIX = A_Q + CONV_CH
ODD_IN = 4 * M_WIDTH + 2 * M_HEADS
N_EVEN = (DEPTH + 1) // 2
N_ODD = DEPTH // 2
EPS = 1e-6

kernel_name = 'hybrid_streaming_swa_conformer_mlstm_step'


def rmsnorm(x, g):
    xf = x.astype(jnp.float32)
    y = xf * lax.rsqrt(jnp.mean(xf * xf, axis=-1, keepdims=True) + EPS)
    return (y * g.astype(jnp.float32)).astype(x.dtype)


def layernorm(x, g, b):
    xf = x.astype(jnp.float32)
    xc = xf - jnp.mean(xf, axis=-1, keepdims=True)
    y = xc * lax.rsqrt(jnp.mean(xc * xc, axis=-1, keepdims=True) + EPS)
    return (y * g.astype(jnp.float32) + b.astype(jnp.float32)).astype(x.dtype)


def partial_rope(x, pos):
    half = ROPE_DIM // 2
    inv = ROPE_THETA ** (-jnp.arange(half, dtype=jnp.float32) / half)
    ang = pos.astype(jnp.float32)[:, None] * inv[None, :]
    cos = jnp.cos(ang)[:, None, :]
    sin = jnp.sin(ang)[:, None, :]
    xr = x[..., :ROPE_DIM].astype(jnp.float32)
    x1, x2 = xr[..., :half], xr[..., half:]
    rot = jnp.concatenate([x1 * cos - x2 * sin, x2 * cos + x1 * sin], axis=-1)
    return jnp.concatenate([rot.astype(x.dtype), x[..., ROPE_DIM:]], axis=-1)


def causal_dwconv(u_ext, w, b):
    out = lax.conv_general_dilated(u_ext, w[:, None, :].astype(u_ext.dtype), window_strides=(1,),
                                   padding='VALID', dimension_numbers=('NWC', 'WIO', 'NWC'),
                                   feature_group_count=u_ext.shape[-1])
    return out + b.astype(out.dtype)


def sink_attention(q, k, v, mask, sink):
    b, nb, lq, _, dh = q.shape
    qg = q.reshape(b, nb, lq, A_KV_HEADS, A_GROUP, dh)
    s = jnp.einsum('bnqhgd,bnkhd->bnhgqk', qg, k, preferred_element_type=jnp.float32) * (HEAD_DIM ** -0.5)
    s = jnp.where(mask[None, :, None, None], s, -jnp.inf)
    sk = sink.astype(jnp.float32).reshape(A_KV_HEADS, A_GROUP, 1, 1)
    mx = jnp.maximum(jnp.max(s, axis=-1, keepdims=True), sk)
    p = jnp.exp(s - mx)
    p = p / (jnp.sum(p, axis=-1, keepdims=True) + jnp.exp(sk - mx))
    o = jnp.einsum('bnhgqk,bnkhd->bnqhgd', p.astype(v.dtype), v)
    return o.reshape(b, nb, lq, A_HEADS, dh)


def swa_prompt(q, k, v, sink):
    b, s_len, h, dh = q.shape
    nc = s_len // CHUNK
    pad = WIN_CHUNKS * CHUNK
    kp = jnp.pad(k, ((0, 0), (pad, 0), (0, 0), (0, 0)))
    vp = jnp.pad(v, ((0, 0), (pad, 0), (0, 0), (0, 0)))
    kb = jnp.concatenate([kp[:, j * CHUNK:j * CHUNK + s_len].reshape(b, nc, CHUNK, A_KV_HEADS, dh)
                          for j in range(WIN_CHUNKS + 1)], axis=2)
    vb = jnp.concatenate([vp[:, j * CHUNK:j * CHUNK + s_len].reshape(b, nc, CHUNK, A_KV_HEADS, dh)
                          for j in range(WIN_CHUNKS + 1)], axis=2)
    n_keys = (WIN_CHUNKS + 1) * CHUNK
    cidx = jnp.arange(nc)[:, None, None]
    kblk = (jnp.arange(n_keys) // CHUNK)[None, None, :]
    mask = jnp.broadcast_to(cidx + kblk - WIN_CHUNKS >= 0, (nc, CHUNK, n_keys))
    o = sink_attention(q.reshape(b, nc, CHUNK, h, dh), kb, vb, mask, sink)
    return o.reshape(b, s_len, h, dh)


def swa_sample(q, k_all, v_all, sink):
    t = q.shape[1]
    mask = jnp.ones((1, t, k_all.shape[1]), dtype=bool)
    return sink_attention(q[:, None], k_all[:, None], v_all[:, None], mask, sink)[:, 0]


def even_mixer(hn, pos, conv_hist, kv_hist, w_in, sink, cw, cb, lng, lnb, w_out):
    bsz, t, _ = hn.shape
    z = hn @ w_in
    q = partial_rope(z[..., :A_Q].reshape(bsz, t, A_HEADS, HEAD_DIM), pos)
    k = partial_rope(z[..., A_Q:A_Q + A_KV].reshape(bsz, t, A_KV_HEADS, HEAD_DIM), pos)
    v = z[..., A_Q + A_KV:A_Q + 2 * A_KV].reshape(bsz, t, A_KV_HEADS, HEAD_DIM)
    glu_a, glu_g = jnp.split(z[..., A_Q + 2 * A_KV:], 2, axis=-1)
    u = glu_a * jax.nn.sigmoid(glu_g)
    if kv_hist is None:
        o = swa_prompt(q, k, v, sink)
        k_all, v_all = k, v
    else:
        k_all = jnp.concatenate([kv_hist[0].astype(k.dtype), k], axis=1)
        v_all = jnp.concatenate([kv_hist[1].astype(v.dtype), v], axis=1)
        o = swa_sample(q, k_all, v_all, sink)
    u_ext = jnp.concatenate([conv_hist.astype(u.dtype), u], axis=1)
    c = jax.nn.silu(layernorm(causal_dwconv(u_ext, cw, cb), lng, lnb))
    y = jnp.concatenate([o.reshape(bsz, t, A_Q), c], axis=-1) @ w_out
    return y, k_all[:, -WINDOW:], v_all[:, -WINDOW:], u_ext[:, -(CONV_W - 1):]


def mlstm_chunk(carry, inp):
    c_st, n_st, m_st = carry
    q, k, v, ig, lf = inp
    length = q.shape[1]
    b = jnp.cumsum(lf, axis=1)
    a = b + m_st[:, None, :]
    dmat = b[:, :, None, :] - b[:, None, :, :] + ig[:, None, :, :]
    causal = jnp.tril(jnp.ones((length, length), dtype=bool))
    dmat = jnp.where(causal[None, :, :, None], dmat, -jnp.inf)
    m_t = jnp.maximum(a, jnp.max(dmat, axis=2))
    w_inter = jnp.exp(a - m_t)
    s = jnp.einsum('bthd,bshd->btsh', q, k) * jnp.exp(dmat - m_t[:, :, None, :])
    num = jnp.einsum('btsh,bshe->bthe', s, v) + w_inter[..., None] * jnp.einsum('bhed,bthd->bthe', c_st, q)
    den = jnp.sum(s, axis=2) + w_inter * jnp.einsum('bhd,bthd->bth', n_st, q)
    h = num / jnp.maximum(jnp.abs(den), jnp.exp(-m_t))[..., None]
    b_end = b[:, -1]
    dec = b_end[:, None, :] - b + ig
    m_new = jnp.maximum(b_end + m_st, jnp.max(dec, axis=1))
    w_s = jnp.exp(dec - m_new[:, None, :])
    w_c = jnp.exp(b_end + m_st - m_new)
    c_new = w_c[..., None, None] * c_st + jnp.einsum('bsh,bshe,bshd->bhed', w_s, v, k)
    n_new = w_c[..., None] * n_st + jnp.einsum('bsh,bshd->bhd', w_s, k)
    return (c_new, n_new, m_new), h


def mlstm_mixer(hn, qk_hist, c0, n0, m0, chunked, w_in, qkc_w, qkc_b, b_i, b_f, norm_g, w_out):
    bsz, t, _ = hn.shape
    z = hn @ w_in
    qk_ext = jnp.concatenate([qk_hist.astype(z.dtype), z[..., :2 * M_WIDTH]], axis=1)
    qk = jax.nn.silu(causal_dwconv(qk_ext, qkc_w, qkc_b)).astype(jnp.float32)
    q = qk[..., :M_WIDTH].reshape(bsz, t, M_HEADS, M_DIM)
    k = qk[..., M_WIDTH:].reshape(bsz, t, M_HEADS, M_DIM) * (M_DIM ** -0.5)
    v = z[..., 2 * M_WIDTH:3 * M_WIDTH].astype(jnp.float32).reshape(bsz, t, M_HEADS, M_DIM)
    o_gate = jax.nn.sigmoid(z[..., 3 * M_WIDTH:4 * M_WIDTH].astype(jnp.float32))
    gates = z[..., 4 * M_WIDTH:].astype(jnp.float32)
    ig = gates[..., :M_HEADS] + b_i.astype(jnp.float32)
    lf = jax.nn.log_sigmoid(gates[..., M_HEADS:] + b_f.astype(jnp.float32))
    carry = (c0.astype(jnp.float32), n0.astype(jnp.float32), m0.astype(jnp.float32))
    if chunked:
        nc = t // CHUNK

        def to_chunks(a):
            return jnp.swapaxes(a.reshape((bsz, nc, CHUNK) + a.shape[2:]), 0, 1)

        carry, h = lax.scan(mlstm_chunk, carry, (to_chunks(q), to_chunks(k), to_chunks(v), to_chunks(ig), to_chunks(lf)))
        h = jnp.swapaxes(h, 0, 1).reshape(bsz, t, M_HEADS, M_DIM)
    else:
        carry, h = mlstm_chunk(carry, (q, k, v, ig, lf))
    h = h * lax.rsqrt(jnp.mean(h * h, axis=-1, keepdims=True) + EPS) * norm_g.astype(jnp.float32).reshape(M_HEADS, M_DIM)
    y = (h.reshape(bsz, t, M_WIDTH) * o_gate).astype(hn.dtype) @ w_out
    return y, qk_ext[:, -(QK_CONV_W - 1):], carry[0], carry[1], carry[2]


def sq_relu_ffn(x, g, w1, w2):
    h = jax.nn.relu(rmsnorm(x, g) @ w1)
    return (h * h) @ w2


def setup_inputs(seed: int = 0) -> dict:
    key = jax.random.key(seed)
    ks = jax.random.split(key, 32)

    def nrm(k, shape, scale):
        return jax.random.normal(k, shape, jnp.float32) * scale

    swa_rows = min(WINDOW, PAST_LEN)
    return {
        'x_prompt': nrm(ks[0], (BATCH, SEQ, D_MODEL), 1.0),
        'x_sample': nrm(ks[1], (DEC_BATCH, DEC_SEQ, D_MODEL), 1.0),
        'cache_swa_k': nrm(ks[2], (N_EVEN, DEC_BATCH, swa_rows, A_KV_HEADS, HEAD_DIM), 1.0),
        'cache_swa_v': nrm(ks[3], (N_EVEN, DEC_BATCH, swa_rows, A_KV_HEADS, HEAD_DIM), 1.0),
        'state_conv': nrm(ks[4], (N_EVEN, DEC_BATCH, CONV_W - 1, CONV_CH), 0.5),
        'state_qk_conv': nrm(ks[5], (N_ODD, DEC_BATCH, QK_CONV_W - 1, 2 * M_WIDTH), 1.0),
        'state_mlstm_C': nrm(ks[6], (N_ODD, DEC_BATCH, M_HEADS, M_DIM, M_DIM), 0.05),
        'state_mlstm_n': nrm(ks[7], (N_ODD, DEC_BATCH, M_HEADS, M_DIM), 0.5),
        'state_mlstm_m': nrm(ks[8], (N_ODD, DEC_BATCH, M_HEADS), 0.5),
        'norm_mix_g': 1.0 + nrm(ks[9], (DEPTH, D_MODEL), 0.02),
        'norm_ffn_g': 1.0 + nrm(ks[10], (DEPTH, D_MODEL), 0.02),
        'norm_out_g': 1.0 + nrm(ks[11], (D_MODEL,), 0.02),
        'w_in_even': nrm(ks[12], (N_EVEN, D_MODEL, EVEN_IN), D_MODEL ** -0.5),
        'attn_sink': nrm(ks[13], (N_EVEN, A_HEADS), 0.5),
        'conv_w': nrm(ks[14], (N_EVEN, CONV_W, CONV_CH), CONV_W ** -0.5),
        'conv_b': nrm(ks[15], (N_EVEN, CONV_CH), 0.02),
        'conv_ln_g': 1.0 + nrm(ks[16], (N_EVEN, CONV_CH), 0.02),
        'conv_ln_b': nrm(ks[17], (N_EVEN, CONV_CH), 0.02),
        'w_out_even': nrm(ks[18], (N_EVEN, EVEN_MIX, D_MODEL), EVEN_MIX ** -0.5),
        'w_in_odd': nrm(ks[19], (N_ODD, D_MODEL, ODD_IN), D_MODEL ** -0.5),
        'qk_conv_w': nrm(ks[20], (N_ODD, QK_CONV_W, 2 * M_WIDTH), QK_CONV_W ** -0.5),
        'qk_conv_b': nrm(ks[21], (N_ODD, 2 * M_WIDTH), 0.02),
        'gate_b_i': nrm(ks[22], (N_ODD, M_HEADS), 0.1),
        'gate_b_f': 3.0 + nrm(ks[23], (N_ODD, M_HEADS), 0.5),
        'mlstm_norm_g': 1.0 + nrm(ks[24], (N_ODD, M_WIDTH), 0.02),
        'w_out_odd': nrm(ks[25], (N_ODD, M_WIDTH, D_MODEL), M_WIDTH ** -0.5),
        'w_ff1': nrm(ks[26], (DEPTH, D_MODEL, D_FF), D_MODEL ** -0.5),
        'w_ff2': nrm(ks[27], (DEPTH, D_FF, D_MODEL), 0.5 * D_FF ** -0.5),
    }


def reference(x_prompt, x_sample, cache_swa_k, cache_swa_v, state_conv, state_qk_conv, state_mlstm_C,
              state_mlstm_n, state_mlstm_m, norm_mix_g, norm_ffn_g, norm_out_g, w_in_even, attn_sink, conv_w,
              conv_b, conv_ln_g, conv_ln_b, w_out_even, w_in_odd, qk_conv_w, qk_conv_b, gate_b_i, gate_b_f,
              mlstm_norm_g, w_out_odd, w_ff1, w_ff2):
    bp, s_len, _ = x_prompt.shape
    bs, t_len, _ = x_sample.shape
    pos_p = jnp.arange(s_len, dtype=jnp.int32)
    pos_s = PAST_LEN + jnp.arange(t_len, dtype=jnp.int32)
    xp, xs = x_prompt, x_sample
    kp_l, vp_l, cvp_l, qkp_l, cp_l, np_l, mp_l = [], [], [], [], [], [], []
    ks_l, vs_l, cvs_l, qks_l, cs_l, ns_l, ms_l = [], [], [], [], [], [], []
    for l in range(DEPTH):
        if l % 2 == 0:
            i = l // 2
            prm = (w_in_even[i], attn_sink[i], conv_w[i], conv_b[i], conv_ln_g[i], conv_ln_b[i], w_out_even[i])
            yp, kp, vp, cvp = even_mixer(rmsnorm(xp, norm_mix_g[l]), pos_p,
                                         jnp.zeros((bp, CONV_W - 1, CONV_CH), xp.dtype), None, *prm)
            ys, ks_, vs_, cvs = even_mixer(rmsnorm(xs, norm_mix_g[l]), pos_s, state_conv[i],
                                           (cache_swa_k[i], cache_swa_v[i]), *prm)
            kp_l.append(kp); vp_l.append(vp); cvp_l.append(cvp)
            ks_l.append(ks_); vs_l.append(vs_); cvs_l.append(cvs)
        else:
            j = l // 2
            prm = (w_in_odd[j], qk_conv_w[j], qk_conv_b[j], gate_b_i[j], gate_b_f[j], mlstm_norm_g[j], w_out_odd[j])
            yp, qkp, cpn, npn, mpn = mlstm_mixer(rmsnorm(xp, norm_mix_g[l]),
                                                 jnp.zeros((bp, QK_CONV_W - 1, 2 * M_WIDTH), xp.dtype),
                                                 jnp.zeros((bp, M_HEADS, M_DIM, M_DIM), jnp.float32),
                                                 jnp.zeros((bp, M_HEADS, M_DIM), jnp.float32),
                                                 jnp.zeros((bp, M_HEADS), jnp.float32), True, *prm)
            ys, qks, csn, nsn, msn = mlstm_mixer(rmsnorm(xs, norm_mix_g[l]), state_qk_conv[j], state_mlstm_C[j],
                                                 state_mlstm_n[j], state_mlstm_m[j], False, *prm)
            qkp_l.append(qkp); cp_l.append(cpn); np_l.append(npn); mp_l.append(mpn)
            qks_l.append(qks); cs_l.append(csn); ns_l.append(nsn); ms_l.append(msn)
        xp = xp + yp
        xs = xs + ys
        xp = xp + sq_relu_ffn(xp, norm_ffn_g[l], w_ff1[l], w_ff2[l])
        xs = xs + sq_relu_ffn(xs, norm_ffn_g[l], w_ff1[l], w_ff2[l])
    y_prompt = rmsnorm(xp, norm_out_g)
    y_sample = rmsnorm(xs, norm_out_g)
    return (y_prompt, y_sample,
            jnp.stack(kp_l), jnp.stack(vp_l), jnp.stack(cvp_l), jnp.stack(qkp_l),
            jnp.stack(cp_l), jnp.stack(np_l), jnp.stack(mp_l),
            jnp.stack(ks_l), jnp.stack(vs_l), jnp.stack(cvs_l), jnp.stack(qks_l),
            jnp.stack(cs_l), jnp.stack(ns_l), jnp.stack(ms_l))
```

```python
import functools

import jax
import jax.numpy as jnp
from jax import lax
from jax.experimental import pallas as pl
from jax.experimental.pallas import tpu as pltpu

F32 = jnp.float32
BF16 = jnp.bfloat16

CHUNK = 64
A_HEADS = 8
A_KV_HEADS = 2
HEAD_DIM = 64
WINDOW = 128
ROPE_DIM = 16
ROPE_THETA = 500000.0
CONV_W = 31
M_HEADS = 4
QK_CONV_W = 4
PAST_LEN = 4096
EPS = 1e-6

LANES = 128
SUBLANES = 8
VMEM_LIMIT_BYTES = 56 * 1024 * 1024

NEG_INF = float("-inf")


def _rms(x, g):
    return x * lax.rsqrt(jnp.mean(x * x, axis=-1, keepdims=True) + EPS) * g


def _sigmoid(x):
    return 1.0 / (1.0 + jnp.exp(-x))


def _log_sigmoid(x):
    return jnp.minimum(x, 0.0) - jnp.log(1.0 + jnp.exp(-jnp.abs(x)))


def _dot(a, b):
    return jnp.dot(a, b, preferred_element_type=F32)


def _dot_nt(a, b):
    return lax.dot_general(a, b, (((1,), (1,)), ((), ())), preferred_element_type=F32)


def _dot_tn(a, b):
    return lax.dot_general(a, b, (((0,), (0,)), ((), ())), preferred_element_type=F32)


def _const_spec(shape):
    nd = len(shape)
    return pl.BlockSpec(shape, lambda *_: (0,) * nd, pipeline_mode=pl.Buffered(1))


def _ffn_body(*refs, n_hidden_chunks, hidden_chunk, final_norm):
    if final_norm:
        x_ref, g_ref, w1_ref, w2_ref, gout_ref, o_ref = refs
    else:
        x_ref, g_ref, w1_ref, w2_ref, o_ref = refs
    x = x_ref[...]
    xb = _rms(x, g_ref[...]).astype(BF16)
    acc = x
    for c in range(n_hidden_chunks):
        cols = slice(c * hidden_chunk, (c + 1) * hidden_chunk)
        h = jnp.maximum(_dot(xb, w1_ref[:, cols]), 0.0)
        acc = acc + _dot((h * h).astype(BF16), w2_ref[cols, :])
    if final_norm:
        acc = _rms(acc, gout_ref[...])
    o_ref[...] = acc


def _ffn(x, g, w1, w2, g_out, *, tile_rows):
    n, d = x.shape
    d_ff = w1.shape[1]
    hidden_chunk = min(d_ff, 1024)
    final_norm = g_out is not None
    tile_rows = min(tile_rows, n)
    assert n % tile_rows == 0 and d_ff % hidden_chunk == 0
    body = functools.partial(_ffn_body, n_hidden_chunks=d_ff // hidden_chunk, hidden_chunk=hidden_chunk,
                             final_norm=final_norm)
    in_specs = [pl.BlockSpec((tile_rows, d), lambda i: (i, 0)), _const_spec((1, d)), _const_spec((d, d_ff)),
                _const_spec((d_ff, d))]
    args = [x, g.reshape(1, d), w1, w2]
    if final_norm:
        in_specs.append(_const_spec((1, d)))
        args.append(g_out.reshape(1, d))
    return pl.pallas_call(
        body,
        grid=(n // tile_rows,),
        in_specs=in_specs,
        out_specs=pl.BlockSpec((tile_rows, d), lambda i: (i, 0)),
        out_shape=jax.ShapeDtypeStruct((n, d), F32),
        compiler_params=pltpu.CompilerParams(dimension_semantics=("arbitrary",), vmem_limit_bytes=VMEM_LIMIT_BYTES),
        name="ffn_final" if final_norm else "ffn",
    )(*args)


CONV_PAD = 32
CONV_ROWS = 64


def _even_body(*refs, tile, q_chunk, has_hist, n_tiles):
    if has_hist:
        (x_ref, g_ref, win_ref, cos_ref, sa_ref, sb_ref, sink_ref, cw_ref, cb_ref, lng_ref, lnb_ref, wout_ref,
         kh_ref, vh_ref, ch_ref, xo_ref, ko_ref, vo_ref, co_ref,
         z_ref, q_ref, kext_ref, vext_ref, kvar_ref, vvar_ref, uext_ref, mix_ref) = refs
    else:
        (x_ref, g_ref, win_ref, cos_ref, sa_ref, sb_ref, sink_ref, cw_ref, cb_ref, lng_ref, lnb_ref, wout_ref,
         xo_ref, ko_ref, vo_ref, co_ref,
         z_ref, q_ref, kext_ref, vext_ref, kvar_ref, vvar_ref, uext_ref, mix_ref) = refs
    t = pl.program_id(1)
    a_q = A_HEADS * HEAD_DIM
    a_kv = A_KV_HEADS * HEAD_DIM
    conv_ch = cw_ref.shape[1]
    n_keys = WINDOW + q_chunk
    hist_rows = CONV_W - 1

    @pl.when(t == 0)
    def _init():
        uext_ref[0:CONV_PAD, :] = jnp.zeros((CONV_PAD, conv_ch), F32)
        if has_hist:
            kext_ref[0:WINDOW, :] = kh_ref[0]
            vext_ref[0:WINDOW, :] = vh_ref[0]
            uext_ref[CONV_PAD - hist_rows:CONV_PAD, :] = ch_ref[0]
        else:
            kext_ref[0:WINDOW, :] = jnp.zeros((WINDOW, a_kv), F32)
            vext_ref[0:WINDOW, :] = jnp.zeros((WINDOW, a_kv), F32)

    x = x_ref[0]
    hn = _rms(x, g_ref[...]).astype(BF16)
    z_ref[...] = _dot(hn, win_ref[...])

    cos_t = cos_ref[...]
    sin_a = sa_ref[...]
    sin_b = sb_ref[...]
    half = ROPE_DIM // 2

    def rope(v):
        return v * cos_t + pltpu.roll(v, LANES - half, 1) * sin_a + pltpu.roll(v, half, 1) * sin_b

    scale = HEAD_DIM ** -0.5
    for j in range(a_q // LANES):
        cols = slice(j * LANES, (j + 1) * LANES)
        q_ref[:, cols] = (rope(z_ref[:, cols]) * scale).astype(BF16)
    kext_ref[WINDOW:WINDOW + tile, :] = rope(z_ref[:, a_q:a_q + a_kv])
    vext_ref[WINDOW:WINDOW + tile, :] = z_ref[:, a_q + a_kv:a_q + 2 * a_kv]

    lane = lax.broadcasted_iota(jnp.int32, (1, LANES), 1)
    low = lane < HEAD_DIM
    for src_ref, var_ref in ((kext_ref, kvar_ref), (vext_ref, vvar_ref)):
        kx = src_ref[...]
        kr = pltpu.roll(kx, HEAD_DIM, 1)
        var_ref[0] = jnp.where(low, kx, 0.0).astype(BF16)
        var_ref[1] = jnp.where(low, 0.0, kr).astype(BF16)
        var_ref[2] = jnp.where(low, kr, 0.0).astype(BF16)
        var_ref[3] = jnp.where(low, 0.0, kx).astype(BF16)

    group = A_HEADS // A_KV_HEADS
    row = lax.broadcasted_iota(jnp.int32, (2 * q_chunk, 1), 0)
    first_rows = row < q_chunk

    def attend(c, carry):
        r0 = c * q_chunk
        if not isinstance(r0, int):
            r0 = pl.multiple_of(r0, q_chunk)
        if not has_hist:
            key_pos = t * tile + r0 - WINDOW + lax.broadcasted_iota(jnp.int32, (1, n_keys), 1)
            bias = jnp.where(key_pos >= 0, 0.0, NEG_INF)
        for g in range(A_KV_HEADS):
            qa = q_ref[pl.ds(r0, q_chunk), 2 * g * LANES:(2 * g + 1) * LANES]
            qb = q_ref[pl.ds(r0, q_chunk), (2 * g + 1) * LANES:(2 * g + 2) * LANES]
            qs = jnp.concatenate([qa, qb], axis=0)
            acc = None
            for hi in range(2):
                s = _dot_nt(qs, kvar_ref[2 * g + hi, pl.ds(r0, n_keys), :])
                if not has_hist:
                    s = s + bias
                sk = jnp.where(first_rows, sink_ref[group * g + hi], sink_ref[group * g + 2 + hi])
                mx = jnp.maximum(jnp.max(s, axis=-1, keepdims=True), sk)
                p = jnp.exp(s - mx)
                den = jnp.sum(p, axis=-1, keepdims=True) + jnp.exp(sk - mx)
                o = _dot(p.astype(BF16), vvar_ref[2 * g + hi, pl.ds(r0, n_keys), :]) * (1.0 / den)
                acc = o if acc is None else acc + o
            mix_ref[pl.ds(r0, q_chunk), 2 * g * LANES:(2 * g + 1) * LANES] = acc[0:q_chunk].astype(BF16)
            mix_ref[pl.ds(r0, q_chunk), (2 * g + 1) * LANES:(2 * g + 2) * LANES] = acc[q_chunk:].astype(BF16)
        return carry

    n_chunks = tile // q_chunk
    if n_chunks == 1:
        attend(0, 0)
    else:
        lax.fori_loop(0, n_chunks, attend, 0)

    glu0 = a_q + 2 * a_kv
    uext_ref[CONV_PAD:CONV_PAD + tile, :] = z_ref[:, glu0:glu0 + conv_ch] * _sigmoid(z_ref[:, glu0 + conv_ch:])
    conv_rows = min(CONV_ROWS, tile)

    def conv_block(i, carry):
        r0 = i * conv_rows
        if not isinstance(r0, int):
            r0 = pl.multiple_of(r0, conv_rows)
        acc = jnp.zeros((conv_rows, conv_ch), F32)
        for w in range(CONV_W):
            acc = acc + uext_ref[pl.ds(r0 + CONV_PAD - hist_rows + w, conv_rows), :] * cw_ref[w:w + 1, :]
        acc = acc + cb_ref[...]
        xc = acc - jnp.mean(acc, axis=-1, keepdims=True)
        y = xc * lax.rsqrt(jnp.mean(xc * xc, axis=-1, keepdims=True) + EPS) * lng_ref[...] + lnb_ref[...]
        mix_ref[pl.ds(r0, conv_rows), a_q:a_q + conv_ch] = (y * _sigmoid(y)).astype(BF16)
        return carry

    for i in range(tile // conv_rows):
        conv_block(i, 0)

    xo_ref[0] = x + _dot(mix_ref[...], wout_ref[...])

    ko_ref[0] = kext_ref[tile:tile + WINDOW, :]
    vo_ref[0] = vext_ref[tile:tile + WINDOW, :]
    co_ref[0] = uext_ref[tile + CONV_PAD - hist_rows:tile + CONV_PAD, :]
    if n_tiles > 1:
        kext_ref[0:WINDOW, :] = kext_ref[tile:tile + WINDOW, :]
        vext_ref[0:WINDOW, :] = vext_ref[tile:tile + WINDOW, :]
        uext_ref[0:CONV_PAD, :] = uext_ref[tile:tile + CONV_PAD, :]


def _rope_tables(pos0, t_len):
    half = ROPE_DIM // 2
    pos = pos0 + jnp.arange(t_len, dtype=jnp.int32)
    inv = ROPE_THETA ** (-jnp.arange(half, dtype=F32) / half)
    ang = pos.astype(F32)[:, None] * inv[None, :]
    cos = jnp.cos(ang)
    sin = jnp.sin(ang)
    ones = jnp.ones((t_len, HEAD_DIM - ROPE_DIM), F32)
    zeros = jnp.zeros((t_len, HEAD_DIM - ROPE_DIM), F32)
    zh = jnp.zeros((t_len, half), F32)
    reps = LANES // HEAD_DIM
    cos_t = jnp.tile(jnp.concatenate([cos, cos, ones], axis=1), (1, reps))
    sin_a = jnp.tile(jnp.concatenate([-sin, zh, zeros], axis=1), (1, reps))
    sin_b = jnp.tile(jnp.concatenate([zh, sin, zeros], axis=1), (1, reps))
    return cos_t, sin_a, sin_b


def _even_layer(x, g, w_in, sink, cw, cb, lng, lnb, w_out, hist, pos0, *, tile, q_chunk):
    bsz, t_len, d = x.shape
    tile = min(tile, t_len)
    assert t_len % tile == 0 and tile % q_chunk == 0
    n_tiles = t_len // tile
    has_hist = hist is not None
    assert has_hist or q_chunk == CHUNK
    assert (not has_hist) or n_tiles == 1
    e_in = w_in.shape[1]
    conv_ch = cw.shape[1]
    a_kv = A_KV_HEADS * HEAD_DIM
    a_q = A_HEADS * HEAD_DIM
    hist_rows = CONV_W - 1
    cos_t, sin_a, sin_b = _rope_tables(pos0, t_len)

    tab_spec = pl.BlockSpec((tile, LANES), lambda b, t: (t, 0))
    in_specs = [
        pl.BlockSpec((1, tile, d), lambda b, t: (b, t, 0)),
        _const_spec((1, d)), _const_spec((d, e_in)),
        tab_spec, tab_spec, tab_spec,
        pl.BlockSpec(memory_space=pltpu.SMEM),
        _const_spec((CONV_W, conv_ch)), _const_spec((1, conv_ch)), _const_spec((1, conv_ch)),
        _const_spec((1, conv_ch)), _const_spec((a_q + conv_ch, d)),
    ]
    args = [x, g.reshape(1, d), w_in, cos_t, sin_a, sin_b, sink, cw, cb.reshape(1, conv_ch),
            lng.reshape(1, conv_ch), lnb.reshape(1, conv_ch), w_out]
    if has_hist:
        k_hist, v_hist, c_hist = hist
        in_specs += [pl.BlockSpec((1, WINDOW, a_kv), lambda b, t: (b, 0, 0)),
                     pl.BlockSpec((1, WINDOW, a_kv), lambda b, t: (b, 0, 0)),
                     pl.BlockSpec((1, hist_rows, conv_ch), lambda b, t: (b, 0, 0))]
        args += [k_hist.reshape(bsz, WINDOW, a_kv), v_hist.reshape(bsz, WINDOW, a_kv), c_hist]
    out_specs = [pl.BlockSpec((1, tile, d), lambda b, t: (b, t, 0)),
                 pl.BlockSpec((1, WINDOW, a_kv), lambda b, t: (b, 0, 0)),
                 pl.BlockSpec((1, WINDOW, a_kv), lambda b, t: (b, 0, 0)),
                 pl.BlockSpec((1, hist_rows, conv_ch), lambda b, t: (b, 0, 0))]
    out_shape = [jax.ShapeDtypeStruct((bsz, t_len, d), F32),
                 jax.ShapeDtypeStruct((bsz, WINDOW, a_kv), F32),
                 jax.ShapeDtypeStruct((bsz, WINDOW, a_kv), F32),
                 jax.ShapeDtypeStruct((bsz, hist_rows, conv_ch), F32)]
    scratch = [pltpu.VMEM((tile, e_in), F32),
               pltpu.VMEM((tile, a_q), BF16),
               pltpu.VMEM((WINDOW + tile, a_kv), F32),
               pltpu.VMEM((WINDOW + tile, a_kv), F32),
               pltpu.VMEM((4, WINDOW + tile, LANES), BF16),
               pltpu.VMEM((4, WINDOW + tile, LANES), BF16),
               pltpu.VMEM((CONV_PAD + tile, conv_ch), F32),
               pltpu.VMEM((tile, a_q + conv_ch), BF16)]
    body = functools.partial(_even_body, tile=tile, q_chunk=q_chunk, has_hist=has_hist, n_tiles=n_tiles)
    xo, ko, vo, co = pl.pallas_call(
        body,
        grid=(bsz, n_tiles),
        in_specs=in_specs,
        out_specs=out_specs,
        out_shape=out_shape,
        scratch_shapes=scratch,
        compiler_params=pltpu.CompilerParams(dimension_semantics=("arbitrary", "arbitrary"),
                                             vmem_limit_bytes=VMEM_LIMIT_BYTES),
        name="even_mixer_hist" if has_hist else "even_mixer",
    )(*args)
    kv_shape = (bsz, WINDOW, A_KV_HEADS, HEAD_DIM)
    return xo, ko.reshape(kv_shape), vo.reshape(kv_shape), co


QK_PAD = 8
GATE_LANE0 = M_HEADS


def _cumsum_rows(x):
    n = x.shape[0]
    tri = (lax.broadcasted_iota(jnp.int32, (n, n), 0) >= lax.broadcasted_iota(jnp.int32, (n, n), 1)).astype(BF16)
    x1 = x.astype(BF16)
    r1 = x - x1.astype(F32)
    x2 = r1.astype(BF16)
    x3 = (r1 - x2.astype(F32)).astype(BF16)
    return _dot(tri, x1) + _dot(tri, x2) + _dot(tri, x3)


def _odd_body(*refs, tile, m_chunk, has_hist, n_tiles):
    if has_hist:
        (x_ref, g_ref, win_ref, wg_ref, gb_ref, cw_ref, cb_ref, ng_ref, wout_ref,
         qkh_ref, c0_ref, n0_ref, m0_ref, xo_ref, qko_ref, co_ref, no_ref, mo_ref,
         qkext_ref, z_ref, qs_ref, ks_ref, mix_ref, c_ref, n_ref, m_ref) = refs
    else:
        (x_ref, g_ref, win_ref, wg_ref, gb_ref, cw_ref, cb_ref, ng_ref, wout_ref,
         xo_ref, qko_ref, co_ref, no_ref, mo_ref,
         qkext_ref, z_ref, qs_ref, ks_ref, mix_ref, c_ref, n_ref, m_ref) = refs
    t = pl.program_id(1)
    width = ng_ref.shape[1]
    m_dim = width // M_HEADS
    hist_rows = QK_CONV_W - 1
    length = m_chunk

    @pl.when(t == 0)
    def _init():
        qkext_ref[0:QK_PAD, :] = jnp.zeros((QK_PAD, 2 * width), F32)
        if has_hist:
            qkext_ref[QK_PAD - hist_rows:QK_PAD, :] = qkh_ref[0]
            c_ref[...] = c0_ref[0]
            n_ref[...] = n0_ref[0]
            m_ref[...] = m0_ref[0]
        else:
            c_ref[...] = jnp.zeros(c_ref.shape, F32)
            n_ref[...] = jnp.zeros(n_ref.shape, F32)
            m_ref[...] = jnp.zeros(m_ref.shape, F32)

    x = x_ref[0]
    hn = _rms(x, g_ref[...]).astype(BF16)
    qkext_ref[QK_PAD:QK_PAD + tile, :] = _dot(hn, win_ref[:, 0:2 * width])
    z_ref[...] = _dot(hn, win_ref[:, 2 * width:4 * width])
    gts = _dot(hn, wg_ref[...]) + gb_ref[...]

    conv_rows = min(64, tile)
    col_blk = 512
    k_scale = m_dim ** -0.5
    for cbk in range(2 * width // col_blk):
        cols = slice(cbk * col_blk, (cbk + 1) * col_blk)
        dst_ref = qs_ref if cbk * col_blk < width else ks_ref
        dcols = slice((cbk * col_blk) % width, (cbk * col_blk) % width + col_blk)
        post = 1.0 if cbk * col_blk < width else k_scale

        def qk_block(i, carry, cols=cols, dst_ref=dst_ref, dcols=dcols, post=post):
            r0 = i * conv_rows
            if not isinstance(r0, int):
                r0 = pl.multiple_of(r0, conv_rows)
            acc = jnp.zeros((conv_rows, col_blk), F32)
            for w in range(QK_CONV_W):
                acc = acc + qkext_ref[pl.ds(r0 + QK_PAD - hist_rows + w, conv_rows), cols] * cw_ref[w:w + 1, cols]
            acc = acc + cb_ref[:, cols]
            y = acc * _sigmoid(acc)
            if post != 1.0:
                y = y * post
            dst_ref[pl.ds(r0, conv_rows), dcols] = y.astype(BF16)
            return carry

        for i in range(tile // conv_rows):
            qk_block(i, 0)

    lane = lax.broadcasted_iota(jnp.int32, (1, LANES), 1)
    gate_lanes = (lane >= GATE_LANE0) & (lane < GATE_LANE0 + M_HEADS)
    lf_all = jnp.where(gate_lanes, _log_sigmoid(gts), 0.0)
    ig_all = jnp.where(gate_lanes, pltpu.roll(gts, GATE_LANE0, 1), 0.0)
    causal = (lax.broadcasted_iota(jnp.int32, (length, length), 0)
              >= lax.broadcasted_iota(jnp.int32, (length, length), 1))
    pad_rows = LANES - length if length < LANES else 0

    for c in range(tile // length):
        rows = slice(c * length, (c + 1) * length)
        lf = lf_all[rows]
        ig = ig_all[rows]
        bcum = _cumsum_rows(lf)
        m_prev = m_ref[...]
        a_all = bcum + m_prev
        b_end = bcum[length - 1:length, :]
        r_all = ig - bcum
        dec = b_end + r_all
        m_new = jnp.maximum(b_end + m_prev, jnp.max(dec, axis=0, keepdims=True))
        w_s_all = jnp.exp(dec - m_new)
        w_c_all = jnp.exp(b_end + m_prev - m_new)
        r_pad = r_all if pad_rows == 0 else jnp.concatenate([r_all, jnp.zeros((pad_rows, LANES), F32)], axis=0)
        r_t = r_pad.T

        for h in range(M_HEADS):
            gl = GATE_LANE0 + h
            hcols = slice(h * m_dim, (h + 1) * m_dim)
            qh = qs_ref[rows, hcols]
            kh = ks_ref[rows, hcols]
            vh = z_ref[rows, hcols]
            og = _sigmoid(z_ref[rows, width + h * m_dim:width + (h + 1) * m_dim])
            c_h = c_ref[h]
            n_h = n_ref[h:h + 1, :]
            dmat = jnp.where(causal, bcum[:, gl:gl + 1] + r_t[gl:gl + 1, 0:length], NEG_INF)
            a_h = a_all[:, gl:gl + 1]
            m_t = jnp.maximum(a_h, jnp.max(dmat, axis=-1, keepdims=True))
            w_inter = jnp.exp(a_h - m_t)
            s = _dot_nt(qh, kh) * jnp.exp(dmat - m_t)
            num = _dot(s.astype(BF16), vh.astype(BF16)) + w_inter * _dot_nt(qh, c_h.astype(BF16))
            qn = jnp.sum(qh.astype(F32) * n_h, axis=-1, keepdims=True)
            den = jnp.sum(s, axis=-1, keepdims=True) + w_inter * qn
            hh = num * (1.0 / jnp.maximum(jnp.abs(den), jnp.exp(-m_t)))
            hh = hh * lax.rsqrt(jnp.mean(hh * hh, axis=-1, keepdims=True) + EPS) * ng_ref[:, hcols]
            mix_ref[rows, hcols] = (hh * og).astype(BF16)
            w_s = w_s_all[:, gl:gl + 1]
            w_c = w_c_all[:, gl:gl + 1]
            c_ref[h] = w_c * c_h + _dot_tn((vh * w_s).astype(BF16), kh)
            n_ref[h:h + 1, :] = w_c * n_h + jnp.sum(kh.astype(F32) * w_s, axis=0, keepdims=True)
        m_ref[...] = jnp.where(gate_lanes, m_new, 0.0)

    xo_ref[0] = x + _dot(mix_ref[...], wout_ref[...])

    qko_ref[0] = qkext_ref[tile + QK_PAD - hist_rows:tile + QK_PAD, :]
    co_ref[0] = c_ref[...]
    no_ref[0] = n_ref[...]
    mo_ref[0] = m_ref[...]
    if n_tiles > 1:
        qkext_ref[0:QK_PAD, :] = qkext_ref[tile:tile + QK_PAD, :]


def _odd_layer(x, g, w_in, w_gate, gate_bias, cw, cb, ng, w_out, hist, *, tile, m_chunk):
    bsz, t_len, d = x.shape
    tile = min(tile, t_len)
    m_chunk = min(m_chunk, tile)
    assert t_len % tile == 0 and tile % m_chunk == 0
    n_tiles = t_len // tile
    has_hist = hist is not None
    assert (not has_hist) or n_tiles == 1
    width = ng.shape[0]
    m_dim = width // M_HEADS
    hist_rows = QK_CONV_W - 1

    in_specs = [
        pl.BlockSpec((1, tile, d), lambda b, t: (b, t, 0)),
        _const_spec((1, d)), _const_spec((d, 4 * width)), _const_spec((d, LANES)), _const_spec((1, LANES)),
        _const_spec((QK_CONV_W, 2 * width)), _const_spec((1, 2 * width)), _const_spec((1, width)),
        _const_spec((width, d)),
    ]
    args = [x, g.reshape(1, d), w_in, w_gate, gate_bias, cw, cb.reshape(1, 2 * width), ng.reshape(1, width), w_out]
    if has_hist:
        qk_hist, c0, n0, m0 = hist
        m0_lanes = jnp.pad(m0, ((0, 0), (GATE_LANE0, LANES - GATE_LANE0 - M_HEADS))).reshape(bsz, 1, LANES)
        in_specs += [pl.BlockSpec((1, hist_rows, 2 * width), lambda b, t: (b, 0, 0)),
                     pl.BlockSpec((1, M_HEADS, m_dim, m_dim), lambda b, t: (b, 0, 0, 0)),
                     pl.BlockSpec((1, M_HEADS, m_dim), lambda b, t: (b, 0, 0)),
                     pl.BlockSpec((1, 1, LANES), lambda b, t: (b, 0, 0))]
        args += [qk_hist, c0, n0, m0_lanes]
    out_specs = [pl.BlockSpec((1, tile, d), lambda b, t: (b, t, 0)),
                 pl.BlockSpec((1, hist_rows, 2 * width), lambda b, t: (b, 0, 0)),
                 pl.BlockSpec((1, M_HEADS, m_dim, m_dim), lambda b, t: (b, 0, 0, 0)),
                 pl.BlockSpec((1, M_HEADS, m_dim), lambda b, t: (b, 0, 0)),
                 pl.BlockSpec((1, 1, LANES), lambda b, t: (b, 0, 0))]
    out_shape = [jax.ShapeDtypeStruct((bsz, t_len, d), F32),
                 jax.ShapeDtypeStruct((bsz, hist_rows, 2 * width), F32),
                 jax.ShapeDtypeStruct((bsz, M_HEADS, m_dim, m_dim), F32),
                 jax.ShapeDtypeStruct((bsz, M_HEADS, m_dim), F32),
                 jax.ShapeDtypeStruct((bsz, 1, LANES), F32)]
    scratch = [pltpu.VMEM((QK_PAD + tile, 2 * width), F32),
               pltpu.VMEM((tile, 2 * width), F32),
               pltpu.VMEM((tile, width), BF16),
               pltpu.VMEM((tile, width), BF16),
               pltpu.VMEM((tile, width), BF16),
               pltpu.VMEM((M_HEADS, m_dim, m_dim), F32),
               pltpu.VMEM((M_HEADS, m_dim), F32),
               pltpu.VMEM((1, LANES), F32)]
    body = functools.partial(_odd_body, tile=tile, m_chunk=m_chunk, has_hist=has_hist, n_tiles=n_tiles)
    xo, qko, co, no, mo = pl.pallas_call(
        body,
        grid=(bsz, n_tiles),
        in_specs=in_specs,
        out_specs=out_specs,
        out_shape=out_shape,
        scratch_shapes=scratch,
        compiler_params=pltpu.CompilerParams(dimension_semantics=("arbitrary", "arbitrary"),
                                             vmem_limit_bytes=VMEM_LIMIT_BYTES),
        name="mlstm_mixer_hist" if has_hist else "mlstm_mixer",
    )(*args)
    return xo, qko, co, no, mo[:, 0, GATE_LANE0:GATE_LANE0 + M_HEADS]


PROMPT_TILE = 512
MLSTM_CHUNK = 128
FFN_ROWS = 512


def kernel(x_prompt, x_sample, cache_swa_k, cache_swa_v, state_conv, state_qk_conv, state_mlstm_C, state_mlstm_n,
           state_mlstm_m, norm_mix_g, norm_ffn_g, norm_out_g, w_in_even, attn_sink, conv_w, conv_b, conv_ln_g,
           conv_ln_b, w_out_even, w_in_odd, qk_conv_w, qk_conv_b, gate_b_i, gate_b_f, mlstm_norm_g, w_out_odd,
           w_ff1, w_ff2):
    depth = norm_mix_g.shape[0]
    bp, s_len, d = x_prompt.shape
    bs, t_len, _ = x_sample.shape
    width = mlstm_norm_g.shape[1]
    xp, xs = x_prompt, x_sample
    outs_p = [[] for _ in range(7)]
    outs_s = [[] for _ in range(7)]
    for l in range(depth):
        if l % 2 == 0:
            i = l // 2
            prm = (norm_mix_g[l], w_in_even[i].astype(BF16), attn_sink[i], conv_w[i], conv_b[i], conv_ln_g[i],
                   conv_ln_b[i], w_out_even[i].astype(BF16))
            xp, kp, vp, cvp = _even_layer(xp, *prm, None, 0, tile=PROMPT_TILE, q_chunk=CHUNK)
            xs, ks_, vs_, cvs = _even_layer(xs, *prm, (cache_swa_k[i], cache_swa_v[i], state_conv[i]), PAST_LEN,
                                            tile=t_len, q_chunk=t_len)
            for lst, val in zip(outs_p[:3], (kp, vp, cvp)):
                lst.append(val)
            for lst, val in zip(outs_s[:3], (ks_, vs_, cvs)):
                lst.append(val)
        else:
            j = l // 2
            w_in = w_in_odd[j]
            n_gate = 2 * M_HEADS
            w_gate = jnp.pad(w_in[:, 4 * width:], ((0, 0), (0, LANES - n_gate))).astype(BF16)
            gate_bias = jnp.pad(jnp.concatenate([gate_b_i[j], gate_b_f[j]]), (0, LANES - n_gate)).reshape(1, LANES)
            prm = (norm_mix_g[l], w_in[:, :4 * width].astype(BF16), w_gate, gate_bias, qk_conv_w[j], qk_conv_b[j],
                   mlstm_norm_g[j], w_out_odd[j].astype(BF16))
            xp, qkp, cpn, npn, mpn = _odd_layer(xp, *prm, None, tile=PROMPT_TILE, m_chunk=MLSTM_CHUNK)
            xs, qks, csn, nsn, msn = _odd_layer(
                xs, *prm, (state_qk_conv[j], state_mlstm_C[j], state_mlstm_n[j], state_mlstm_m[j]),
                tile=t_len, m_chunk=t_len)
            for lst, val in zip(outs_p[3:], (qkp, cpn, npn, mpn)):
                lst.append(val)
            for lst, val in zip(outs_s[3:], (qks, csn, nsn, msn)):
                lst.append(val)
        g_out = norm_out_g if l == depth - 1 else None
        w1 = w_ff1[l].astype(BF16)
        w2 = w_ff2[l].astype(BF16)
        xp = _ffn(xp.reshape(bp * s_len, d), norm_ffn_g[l], w1, w2, g_out, tile_rows=FFN_ROWS).reshape(bp, s_len, d)
        xs = _ffn(xs.reshape(bs * t_len, d), norm_ffn_g[l], w1, w2, g_out, tile_rows=FFN_ROWS).reshape(bs, t_len, d)
    return (xp, xs) + tuple(jnp.stack(v) for v in outs_p) + tuple(jnp.stack(v) for v in outs_s)
```

```python
import functools

import jax
import jax.numpy as jnp
from jax import lax
from jax.experimental import pallas as pl
from jax.experimental.pallas import tpu as pltpu

F32 = jnp.float32
BF16 = jnp.bfloat16

CHUNK = 64
A_HEADS = 8
A_KV_HEADS = 2
HEAD_DIM = 64
WINDOW = 128
ROPE_DIM = 16
ROPE_THETA = 500000.0
CONV_W = 31
M_HEADS = 4
QK_CONV_W = 4
PAST_LEN = 4096
EPS = 1e-6

LANES = 128
SUBLANES = 8
VMEM_LIMIT_BYTES = 56 * 1024 * 1024

NEG_INF = float("-inf")


def _rms(x, g):
    return x * lax.rsqrt(jnp.mean(x * x, axis=-1, keepdims=True) + EPS) * g


def _sigmoid(x):
    return 1.0 / (1.0 + jnp.exp(-x))


def _log_sigmoid(x):
    return jnp.minimum(x, 0.0) - jnp.log(1.0 + jnp.exp(-jnp.abs(x)))


def _dot(a, b):
    return jnp.dot(a, b, preferred_element_type=F32)


def _dot_nt(a, b):
    return lax.dot_general(a, b, (((1,), (1,)), ((), ())), preferred_element_type=F32)


def _dot_tn(a, b):
    return lax.dot_general(a, b, (((0,), (0,)), ((), ())), preferred_element_type=F32)


def _const_spec(shape):
    nd = len(shape)
    return pl.BlockSpec(shape, lambda *_: (0,) * nd, pipeline_mode=pl.Buffered(1))


def _ffn_body(*refs, n_hidden_chunks, hidden_chunk, final_norm):
    if final_norm:
        x_ref, g_ref, w1_ref, w2_ref, gout_ref, o_ref = refs
    else:
        x_ref, g_ref, w1_ref, w2_ref, o_ref = refs
    x = x_ref[...]
    xb = _rms(x, g_ref[...]).astype(BF16)
    acc = x
    for c in range(n_hidden_chunks):
        cols = slice(c * hidden_chunk, (c + 1) * hidden_chunk)
        h = jnp.maximum(_dot(xb, w1_ref[:, cols]), 0.0)
        acc = acc + _dot((h * h).astype(BF16), w2_ref[cols, :])
    if final_norm:
        acc = _rms(acc, gout_ref[...])
    o_ref[...] = acc


def _ffn(x, g, w1, w2, g_out, *, tile_rows):
    n, d = x.shape
    d_ff = w1.shape[1]
    hidden_chunk = min(d_ff, 1024)
    final_norm = g_out is not None
    tile_rows = min(tile_rows, n)
    assert n % tile_rows == 0 and d_ff % hidden_chunk == 0
    body = functools.partial(_ffn_body, n_hidden_chunks=d_ff // hidden_chunk, hidden_chunk=hidden_chunk,
                             final_norm=final_norm)
    in_specs = [pl.BlockSpec((tile_rows, d), lambda i: (i, 0)), _const_spec((1, d)), _const_spec((d, d_ff)),
                _const_spec((d_ff, d))]
    args = [x, g.reshape(1, d), w1, w2]
    if final_norm:
        in_specs.append(_const_spec((1, d)))
        args.append(g_out.reshape(1, d))
    return pl.pallas_call(
        body,
        grid=(n // tile_rows,),
        in_specs=in_specs,
        out_specs=pl.BlockSpec((tile_rows, d), lambda i: (i, 0)),
        out_shape=jax.ShapeDtypeStruct((n, d), F32),
        compiler_params=pltpu.CompilerParams(dimension_semantics=("arbitrary",), vmem_limit_bytes=VMEM_LIMIT_BYTES),
        name="ffn_final" if final_norm else "ffn",
    )(*args)


CONV_PAD = 32
CONV_ROWS = 64


def _even_body(*refs, tile, q_chunk, has_hist, n_tiles):
    if has_hist:
        (x_ref, g_ref, win_ref, cos_ref, sa_ref, sb_ref, sink_ref, cw_ref, cb_ref, lng_ref, lnb_ref, wout_ref,
         kh_ref, vh_ref, ch_ref, xo_ref, ko_ref, vo_ref, co_ref,
         z_ref, q_ref, kext_ref, vext_ref, kvar_ref, vvar_ref, uext_ref, mix_ref) = refs
    else:
        (x_ref, g_ref, win_ref, cos_ref, sa_ref, sb_ref, sink_ref, cw_ref, cb_ref, lng_ref, lnb_ref, wout_ref,
         xo_ref, ko_ref, vo_ref, co_ref,
         z_ref, q_ref, kext_ref, vext_ref, kvar_ref, vvar_ref, uext_ref, mix_ref) = refs
    t = pl.program_id(1)
    a_q = A_HEADS * HEAD_DIM
    a_kv = A_KV_HEADS * HEAD_DIM
    conv_ch = cw_ref.shape[1]
    n_slab = conv_ch // LANES
    n_keys = WINDOW + q_chunk
    hist_rows = CONV_W - 1

    @pl.when(t == 0)
    def _init():
        for j in range(n_slab):
            uext_ref[j, 0:CONV_PAD, :] = jnp.zeros((CONV_PAD, LANES), F32)
        if has_hist:
            kext_ref[0:WINDOW, :] = kh_ref[0]
            vext_ref[0:WINDOW, :] = vh_ref[0]
            for j in range(n_slab):
                uext_ref[j, CONV_PAD - hist_rows:CONV_PAD, :] = ch_ref[0, :, j * LANES:(j + 1) * LANES]
        else:
            kext_ref[0:WINDOW, :] = jnp.zeros((WINDOW, a_kv), F32)
            vext_ref[0:WINDOW, :] = jnp.zeros((WINDOW, a_kv), F32)

    x = x_ref[0]
    hn = _rms(x, g_ref[...]).astype(BF16)
    z_ref[...] = _dot(hn, win_ref[...])

    cos_t = cos_ref[...]
    sin_a = sa_ref[...]
    sin_b = sb_ref[...]
    half = ROPE_DIM // 2

    def rope(v):
        return v * cos_t + pltpu.roll(v, LANES - half, 1) * sin_a + pltpu.roll(v, half, 1) * sin_b

    scale = HEAD_DIM ** -0.5
    for j in range(a_q // LANES):
        cols = slice(j * LANES, (j + 1) * LANES)
        q_ref[:, cols] = (rope(z_ref[:, cols]) * scale).astype(BF16)
    kext_ref[WINDOW:WINDOW + tile, :] = rope(z_ref[:, a_q:a_q + a_kv])
    vext_ref[WINDOW:WINDOW + tile, :] = z_ref[:, a_q + a_kv:a_q + 2 * a_kv]

    lane = lax.broadcasted_iota(jnp.int32, (1, LANES), 1)
    low = lane < HEAD_DIM
    for src_ref, var_ref in ((kext_ref, kvar_ref), (vext_ref, vvar_ref)):
        kx = src_ref[...]
        kr = pltpu.roll(kx, HEAD_DIM, 1)
        var_ref[0] = jnp.where(low, kx, 0.0).astype(BF16)
        var_ref[1] = jnp.where(low, 0.0, kr).astype(BF16)
        var_ref[2] = jnp.where(low, kr, 0.0).astype(BF16)
        var_ref[3] = jnp.where(low, 0.0, kx).astype(BF16)

    group = A_HEADS // A_KV_HEADS
    row = lax.broadcasted_iota(jnp.int32, (2 * q_chunk, 1), 0)
    first_rows = row < q_chunk

    def attend(c, carry):
        r0 = c * q_chunk
        if not isinstance(r0, int):
            r0 = pl.multiple_of(r0, q_chunk)
        if not has_hist:
            key_pos = t * tile + r0 - WINDOW + lax.broadcasted_iota(jnp.int32, (1, n_keys), 1)
            bias = jnp.where(key_pos >= 0, 0.0, NEG_INF)
        for g in range(A_KV_HEADS):
            qa = q_ref[pl.ds(r0, q_chunk), 2 * g * LANES:(2 * g + 1) * LANES]
            qb = q_ref[pl.ds(r0, q_chunk), (2 * g + 1) * LANES:(2 * g + 2) * LANES]
            qs = jnp.concatenate([qa, qb], axis=0)
            acc = None
            for hi in range(2):
                s = _dot_nt(qs, kvar_ref[2 * g + hi, pl.ds(r0, n_keys), :])
                if not has_hist:
                    s = s + bias
                sk = jnp.where(first_rows, sink_ref[group * g + hi], sink_ref[group * g + 2 + hi])
                mx = jnp.maximum(jnp.max(s, axis=-1, keepdims=True), sk)
                p = jnp.exp(s - mx)
                den = jnp.sum(p, axis=-1, keepdims=True) + jnp.exp(sk - mx)
                o = _dot(p.astype(BF16), vvar_ref[2 * g + hi, pl.ds(r0, n_keys), :]) * (1.0 / den)
                acc = o if acc is None else acc + o
            mix_ref[pl.ds(r0, q_chunk), 2 * g * LANES:(2 * g + 1) * LANES] = acc[0:q_chunk].astype(BF16)
            mix_ref[pl.ds(r0, q_chunk), (2 * g + 1) * LANES:(2 * g + 2) * LANES] = acc[q_chunk:].astype(BF16)
        return carry

    for c in range(tile // q_chunk):
        attend(c, 0)

    glu0 = a_q + 2 * a_kv
    for j in range(n_slab):
        lo = glu0 + j * LANES
        uext_ref[j, CONV_PAD:CONV_PAD + tile, :] = (
            z_ref[:, lo:lo + LANES] * _sigmoid(z_ref[:, lo + conv_ch:lo + conv_ch + LANES]))
    conv_rows = min(CONV_ROWS, tile)
    for i in range(tile // conv_rows):
        r0 = i * conv_rows
        accs = []
        for j in range(n_slab):
            cols = slice(j * LANES, (j + 1) * LANES)
            acc = jnp.zeros((conv_rows, LANES), F32)
            for w in range(CONV_W):
                acc = acc + uext_ref[j, pl.ds(r0 + CONV_PAD - hist_rows + w, conv_rows), :] * cw_ref[w:w + 1, cols]
            accs.append(acc + cb_ref[:, cols])
        mean = sum(jnp.sum(a, axis=-1, keepdims=True) for a in accs) * (1.0 / conv_ch)
        cent = [a - mean for a in accs]
        var = sum(jnp.sum(xc * xc, axis=-1, keepdims=True) for xc in cent) * (1.0 / conv_ch)
        rstd = lax.rsqrt(var + EPS)
        for j in range(n_slab):
            cols = slice(j * LANES, (j + 1) * LANES)
            y = cent[j] * rstd * lng_ref[:, cols] + lnb_ref[:, cols]
            mix_ref[r0:r0 + conv_rows, a_q + j * LANES:a_q + (j + 1) * LANES] = (y * _sigmoid(y)).astype(BF16)

    xo_ref[0] = x + _dot(mix_ref[...], wout_ref[...])

    ko_ref[0] = kext_ref[tile:tile + WINDOW, :]
    vo_ref[0] = vext_ref[tile:tile + WINDOW, :]
    for j in range(n_slab):
        co_ref[0, :, j * LANES:(j + 1) * LANES] = uext_ref[j, tile + CONV_PAD - hist_rows:tile + CONV_PAD, :]
    if n_tiles > 1:
        kext_ref[0:WINDOW, :] = kext_ref[tile:tile + WINDOW, :]
        vext_ref[0:WINDOW, :] = vext_ref[tile:tile + WINDOW, :]
        for j in range(n_slab):
            uext_ref[j, 0:CONV_PAD, :] = uext_ref[j, tile:tile + CONV_PAD, :]


def _rope_tables(pos0, t_len):
    half = ROPE_DIM // 2
    pos = pos0 + jnp.arange(t_len, dtype=jnp.int32)
    inv = ROPE_THETA ** (-jnp.arange(half, dtype=F32) / half)
    ang = pos.astype(F32)[:, None] * inv[None, :]
    cos = jnp.cos(ang)
    sin = jnp.sin(ang)
    ones = jnp.ones((t_len, HEAD_DIM - ROPE_DIM), F32)
    zeros = jnp.zeros((t_len, HEAD_DIM - ROPE_DIM), F32)
    zh = jnp.zeros((t_len, half), F32)
    reps = LANES // HEAD_DIM
    cos_t = jnp.tile(jnp.concatenate([cos, cos, ones], axis=1), (1, reps))
    sin_a = jnp.tile(jnp.concatenate([-sin, zh, zeros], axis=1), (1, reps))
    sin_b = jnp.tile(jnp.concatenate([zh, sin, zeros], axis=1), (1, reps))
    return cos_t, sin_a, sin_b


def _even_layer(x, g, w_in, sink, cw, cb, lng, lnb, w_out, hist, pos0, *, tile, q_chunk):
    bsz, t_len, d = x.shape
    tile = min(tile, t_len)
    assert t_len % tile == 0 and tile % q_chunk == 0
    n_tiles = t_len // tile
    has_hist = hist is not None
    assert has_hist or q_chunk == CHUNK
    assert (not has_hist) or n_tiles == 1
    e_in = w_in.shape[1]
    conv_ch = cw.shape[1]
    a_kv = A_KV_HEADS * HEAD_DIM
    a_q = A_HEADS * HEAD_DIM
    hist_rows = CONV_W - 1
    cos_t, sin_a, sin_b = _rope_tables(pos0, t_len)

    tab_spec = pl.BlockSpec((tile, LANES), lambda b, t: (t, 0))
    in_specs = [
        pl.BlockSpec((1, tile, d), lambda b, t: (b, t, 0)),
        _const_spec((1, d)), _const_spec((d, e_in)),
        tab_spec, tab_spec, tab_spec,
        pl.BlockSpec(memory_space=pltpu.SMEM),
        _const_spec((CONV_W, conv_ch)), _const_spec((1, conv_ch)), _const_spec((1, conv_ch)),
        _const_spec((1, conv_ch)), _const_spec((a_q + conv_ch, d)),
    ]
    args = [x, g.reshape(1, d), w_in, cos_t, sin_a, sin_b, sink, cw, cb.reshape(1, conv_ch),
            lng.reshape(1, conv_ch), lnb.reshape(1, conv_ch), w_out]
    if has_hist:
        k_hist, v_hist, c_hist = hist
        in_specs += [pl.BlockSpec((1, WINDOW, a_kv), lambda b, t: (b, 0, 0)),
                     pl.BlockSpec((1, WINDOW, a_kv), lambda b, t: (b, 0, 0)),
                     pl.BlockSpec((1, hist_rows, conv_ch), lambda b, t: (b, 0, 0))]
        args += [k_hist.reshape(bsz, WINDOW, a_kv), v_hist.reshape(bsz, WINDOW, a_kv), c_hist]
    out_specs = [pl.BlockSpec((1, tile, d), lambda b, t: (b, t, 0)),
                 pl.BlockSpec((1, WINDOW, a_kv), lambda b, t: (b, 0, 0)),
                 pl.BlockSpec((1, WINDOW, a_kv), lambda b, t: (b, 0, 0)),
                 pl.BlockSpec((1, hist_rows, conv_ch), lambda b, t: (b, 0, 0))]
    out_shape = [jax.ShapeDtypeStruct((bsz, t_len, d), F32),
                 jax.ShapeDtypeStruct((bsz, WINDOW, a_kv), F32),
                 jax.ShapeDtypeStruct((bsz, WINDOW, a_kv), F32),
                 jax.ShapeDtypeStruct((bsz, hist_rows, conv_ch), F32)]
    scratch = [pltpu.VMEM((tile, e_in), F32),
               pltpu.VMEM((tile, a_q), BF16),
               pltpu.VMEM((WINDOW + tile, a_kv), F32),
               pltpu.VMEM((WINDOW + tile, a_kv), F32),
               pltpu.VMEM((4, WINDOW + tile, LANES), BF16),
               pltpu.VMEM((4, WINDOW + tile, LANES), BF16),
               pltpu.VMEM((conv_ch // LANES, CONV_PAD + tile, LANES), F32),
               pltpu.VMEM((tile, a_q + conv_ch), BF16)]
    body = functools.partial(_even_body, tile=tile, q_chunk=q_chunk, has_hist=has_hist, n_tiles=n_tiles)
    xo, ko, vo, co = pl.pallas_call(
        body,
        grid=(bsz, n_tiles),
        in_specs=in_specs,
        out_specs=out_specs,
        out_shape=out_shape,
        scratch_shapes=scratch,
        compiler_params=pltpu.CompilerParams(dimension_semantics=("arbitrary", "arbitrary"),
                                             vmem_limit_bytes=VMEM_LIMIT_BYTES),
        name="even_mixer_hist" if has_hist else "even_mixer",
    )(*args)
    kv_shape = (bsz, WINDOW, A_KV_HEADS, HEAD_DIM)
    return xo, ko.reshape(kv_shape), vo.reshape(kv_shape), co


QK_PAD = 8
QK_CONV_ROWS = 128
GATE_LANE0 = M_HEADS


def _cumsum_rows(x):
    n = x.shape[0]
    tri = (lax.broadcasted_iota(jnp.int32, (n, n), 0) >= lax.broadcasted_iota(jnp.int32, (n, n), 1)).astype(BF16)
    x1 = x.astype(BF16)
    r1 = x - x1.astype(F32)
    x2 = r1.astype(BF16)
    x3 = (r1 - x2.astype(F32)).astype(BF16)
    return _dot(tri, x1) + _dot(tri, x2) + _dot(tri, x3)


def _odd_body(*refs, tile, m_chunk, has_hist, n_tiles):
    if has_hist:
        (x_ref, g_ref, win_ref, wg_ref, gb_ref, cw_ref, cb_ref, ng_ref, wout_ref,
         qkh_ref, c0_ref, n0_ref, m0_ref, xo_ref, qko_ref, co_ref, no_ref, mo_ref,
         qkext_ref, z_ref, qs_ref, ks_ref, mix_ref, c_ref, n_ref, m_ref) = refs
    else:
        (x_ref, g_ref, win_ref, wg_ref, gb_ref, cw_ref, cb_ref, ng_ref, wout_ref,
         xo_ref, qko_ref, co_ref, no_ref, mo_ref,
         qkext_ref, z_ref, qs_ref, ks_ref, mix_ref, c_ref, n_ref, m_ref) = refs
    t = pl.program_id(1)
    width = ng_ref.shape[1]
    m_dim = width // M_HEADS
    n_slab = 2 * width // LANES
    hist_rows = QK_CONV_W - 1
    length = m_chunk

    @pl.when(t == 0)
    def _init():
        for j in range(n_slab):
            qkext_ref[j, 0:QK_PAD, :] = jnp.zeros((QK_PAD, LANES), F32)
        if has_hist:
            for j in range(n_slab):
                qkext_ref[j, QK_PAD - hist_rows:QK_PAD, :] = qkh_ref[0, :, j * LANES:(j + 1) * LANES]
            c_ref[...] = c0_ref[0]
            n_ref[...] = n0_ref[0]
            m_ref[...] = m0_ref[0]
        else:
            c_ref[...] = jnp.zeros(c_ref.shape, F32)
            n_ref[...] = jnp.zeros(n_ref.shape, F32)
            m_ref[...] = jnp.zeros(m_ref.shape, F32)

    x = x_ref[0]
    hn = _rms(x, g_ref[...]).astype(BF16)
    for n in range(n_slab // 2):
        r = _dot(hn, win_ref[:, 2 * n * LANES:(2 * n + 2) * LANES])
        qkext_ref[2 * n, QK_PAD:QK_PAD + tile, :] = r[:, 0:LANES]
        qkext_ref[2 * n + 1, QK_PAD:QK_PAD + tile, :] = r[:, LANES:2 * LANES]
    z_ref[...] = _dot(hn, win_ref[:, 2 * width:4 * width])
    gts = _dot(hn, wg_ref[...]) + gb_ref[...]

    conv_rows = min(QK_CONV_ROWS, tile)
    k_scale = m_dim ** -0.5
    for j in range(n_slab):
        cols = slice(j * LANES, (j + 1) * LANES)
        is_q = j * LANES < width
        dst_ref = qs_ref if is_q else ks_ref
        dcols = slice(j * LANES % width, j * LANES % width + LANES)
        for i in range(tile // conv_rows):
            r0 = i * conv_rows
            acc = jnp.zeros((conv_rows, LANES), F32)
            for w in range(QK_CONV_W):
                acc = acc + qkext_ref[j, pl.ds(r0 + QK_PAD - hist_rows + w, conv_rows), :] * cw_ref[w:w + 1, cols]
            acc = acc + cb_ref[:, cols]
            y = acc * _sigmoid(acc)
            if not is_q:
                y = y * k_scale
            dst_ref[r0:r0 + conv_rows, dcols] = y.astype(BF16)

    lane = lax.broadcasted_iota(jnp.int32, (1, LANES), 1)
    gate_lanes = (lane >= GATE_LANE0) & (lane < GATE_LANE0 + M_HEADS)
    lf_all = jnp.where(gate_lanes, _log_sigmoid(gts), 0.0)
    ig_all = jnp.where(gate_lanes, pltpu.roll(gts, GATE_LANE0, 1), 0.0)
    causal = (lax.broadcasted_iota(jnp.int32, (length, length), 0)
              >= lax.broadcasted_iota(jnp.int32, (length, length), 1))
    pad_rows = LANES - length if length < LANES else 0

    for c in range(tile // length):
        rows = slice(c * length, (c + 1) * length)
        lf = lf_all[rows]
        ig = ig_all[rows]
        bcum = _cumsum_rows(lf)
        m_prev = m_ref[...]
        a_all = bcum + m_prev
        b_end = bcum[length - 1:length, :]
        r_all = ig - bcum
        dec = b_end + r_all
        m_new = jnp.maximum(b_end + m_prev, jnp.max(dec, axis=0, keepdims=True))
        w_s_all = jnp.exp(dec - m_new)
        w_c_all = jnp.exp(b_end + m_prev - m_new)
        r_pad = r_all if pad_rows == 0 else jnp.concatenate([r_all, jnp.zeros((pad_rows, LANES), F32)], axis=0)
        r_t = r_pad.T

        for h in range(M_HEADS):
            gl = GATE_LANE0 + h
            hcols = slice(h * m_dim, (h + 1) * m_dim)
            qh = qs_ref[rows, hcols]
            kh = ks_ref[rows, hcols]
            vh = z_ref[rows, hcols]
            og = _sigmoid(z_ref[rows, width + h * m_dim:width + (h + 1) * m_dim])
            c_h = c_ref[h]
            n_h = n_ref[h:h + 1, :]
            dmat = jnp.where(causal, bcum[:, gl:gl + 1] + r_t[gl:gl + 1, 0:length], NEG_INF)
            a_h = a_all[:, gl:gl + 1]
            m_t = jnp.maximum(a_h, jnp.max(dmat, axis=-1, keepdims=True))
            w_inter = jnp.exp(a_h - m_t)
            s = _dot_nt(qh, kh) * jnp.exp(dmat - m_t)
            num = _dot(s.astype(BF16), vh.astype(BF16)) + w_inter * _dot_nt(qh, c_h.astype(BF16))
            qn = jnp.sum(qh.astype(F32) * n_h, axis=-1, keepdims=True)
            den = jnp.sum(s, axis=-1, keepdims=True) + w_inter * qn
            hh = num * (1.0 / jnp.maximum(jnp.abs(den), jnp.exp(-m_t)))
            hh = hh * lax.rsqrt(jnp.mean(hh * hh, axis=-1, keepdims=True) + EPS) * ng_ref[:, hcols]
            mix_ref[rows, hcols] = (hh * og).astype(BF16)
            w_s = w_s_all[:, gl:gl + 1]
            w_c = w_c_all[:, gl:gl + 1]
            c_ref[h] = w_c * c_h + _dot_tn((vh * w_s).astype(BF16), kh)
            n_ref[h:h + 1, :] = w_c * n_h + jnp.sum(kh.astype(F32) * w_s, axis=0, keepdims=True)
        m_ref[...] = jnp.where(gate_lanes, m_new, 0.0)

    xo_ref[0] = x + _dot(mix_ref[...], wout_ref[...])

    for j in range(n_slab):
        qko_ref[0, :, j * LANES:(j + 1) * LANES] = qkext_ref[j, tile + QK_PAD - hist_rows:tile + QK_PAD, :]
    co_ref[0] = c_ref[...]
    no_ref[0] = n_ref[...]
    mo_ref[0] = m_ref[...]
    if n_tiles > 1:
        for j in range(n_slab):
            qkext_ref[j, 0:QK_PAD, :] = qkext_ref[j, tile:tile + QK_PAD, :]


def _odd_layer(x, g, w_in, w_gate, gate_bias, cw, cb, ng, w_out, hist, *, tile, m_chunk):
    bsz, t_len, d = x.shape
    tile = min(tile, t_len)
    m_chunk = min(m_chunk, tile)
    assert t_len % tile == 0 and tile % m_chunk == 0
    n_tiles = t_len // tile
    has_hist = hist is not None
    assert (not has_hist) or n_tiles == 1
    width = ng.shape[0]
    m_dim = width // M_HEADS
    hist_rows = QK_CONV_W - 1

    in_specs = [
        pl.BlockSpec((1, tile, d), lambda b, t: (b, t, 0)),
        _const_spec((1, d)), _const_spec((d, 4 * width)), _const_spec((d, LANES)), _const_spec((1, LANES)),
        _const_spec((QK_CONV_W, 2 * width)), _const_spec((1, 2 * width)), _const_spec((1, width)),
        _const_spec((width, d)),
    ]
    args = [x, g.reshape(1, d), w_in, w_gate, gate_bias, cw, cb.reshape(1, 2 * width), ng.reshape(1, width), w_out]
    if has_hist:
        qk_hist, c0, n0, m0 = hist
        m0_lanes = jnp.pad(m0, ((0, 0), (GATE_LANE0, LANES - GATE_LANE0 - M_HEADS))).reshape(bsz, 1, LANES)
        in_specs += [pl.BlockSpec((1, hist_rows, 2 * width), lambda b, t: (b, 0, 0)),
                     pl.BlockSpec((1, M_HEADS, m_dim, m_dim), lambda b, t: (b, 0, 0, 0)),
                     pl.BlockSpec((1, M_HEADS, m_dim), lambda b, t: (b, 0, 0)),
                     pl.BlockSpec((1, 1, LANES), lambda b, t: (b, 0, 0))]
        args += [qk_hist, c0, n0, m0_lanes]
    out_specs = [pl.BlockSpec((1, tile, d), lambda b, t: (b, t, 0)),
                 pl.BlockSpec((1, hist_rows, 2 * width), lambda b, t: (b, 0, 0)),
                 pl.BlockSpec((1, M_HEADS, m_dim, m_dim), lambda b, t: (b, 0, 0, 0)),
                 pl.BlockSpec((1, M_HEADS, m_dim), lambda b, t: (b, 0, 0)),
                 pl.BlockSpec((1, 1, LANES), lambda b, t: (b, 0, 0))]
    out_shape = [jax.ShapeDtypeStruct((bsz, t_len, d), F32),
                 jax.ShapeDtypeStruct((bsz, hist_rows, 2 * width), F32),
                 jax.ShapeDtypeStruct((bsz, M_HEADS, m_dim, m_dim), F32),
                 jax.ShapeDtypeStruct((bsz, M_HEADS, m_dim), F32),
                 jax.ShapeDtypeStruct((bsz, 1, LANES), F32)]
    scratch = [pltpu.VMEM((2 * width // LANES, QK_PAD + tile, LANES), F32),
               pltpu.VMEM((tile, 2 * width), F32),
               pltpu.VMEM((tile, width), BF16),
               pltpu.VMEM((tile, width), BF16),
               pltpu.VMEM((tile, width), BF16),
               pltpu.VMEM((M_HEADS, m_dim, m_dim), F32),
               pltpu.VMEM((M_HEADS, m_dim), F32),
               pltpu.VMEM((1, LANES), F32)]
    body = functools.partial(_odd_body, tile=tile, m_chunk=m_chunk, has_hist=has_hist, n_tiles=n_tiles)
    xo, qko, co, no, mo = pl.pallas_call(
        body,
        grid=(bsz, n_tiles),
        in_specs=in_specs,
        out_specs=out_specs,
        out_shape=out_shape,
        scratch_shapes=scratch,
        compiler_params=pltpu.CompilerParams(dimension_semantics=("arbitrary", "arbitrary"),
                                             vmem_limit_bytes=VMEM_LIMIT_BYTES),
        name="mlstm_mixer_hist" if has_hist else "mlstm_mixer",
    )(*args)
    return xo, qko, co, no, mo[:, 0, GATE_LANE0:GATE_LANE0 + M_HEADS]


PROMPT_TILE = 512
MLSTM_CHUNK = 128
FFN_ROWS = 512


def kernel(x_prompt, x_sample, cache_swa_k, cache_swa_v, state_conv, state_qk_conv, state_mlstm_C, state_mlstm_n,
           state_mlstm_m, norm_mix_g, norm_ffn_g, norm_out_g, w_in_even, attn_sink, conv_w, conv_b, conv_ln_g,
           conv_ln_b, w_out_even, w_in_odd, qk_conv_w, qk_conv_b, gate_b_i, gate_b_f, mlstm_norm_g, w_out_odd,
           w_ff1, w_ff2):
    depth = norm_mix_g.shape[0]
    bp, s_len, d = x_prompt.shape
    bs, t_len, _ = x_sample.shape
    width = mlstm_norm_g.shape[1]
    xp, xs = x_prompt, x_sample
    outs_p = [[] for _ in range(7)]
    outs_s = [[] for _ in range(7)]
    for l in range(depth):
        if l % 2 == 0:
            i = l // 2
            prm = (norm_mix_g[l], w_in_even[i].astype(BF16), attn_sink[i], conv_w[i], conv_b[i], conv_ln_g[i],
                   conv_ln_b[i], w_out_even[i].astype(BF16))
            xp, kp, vp, cvp = _even_layer(xp, *prm, None, 0, tile=PROMPT_TILE, q_chunk=CHUNK)
            xs, ks_, vs_, cvs = _even_layer(xs, *prm, (cache_swa_k[i], cache_swa_v[i], state_conv[i]), PAST_LEN,
                                            tile=t_len, q_chunk=t_len)
            for lst, val in zip(outs_p[:3], (kp, vp, cvp)):
                lst.append(val)
            for lst, val in zip(outs_s[:3], (ks_, vs_, cvs)):
                lst.append(val)
        else:
            j = l // 2
            w_in = w_in_odd[j]
            n_gate = 2 * M_HEADS
            w_gate = jnp.pad(w_in[:, 4 * width:], ((0, 0), (0, LANES - n_gate))).astype(BF16)
            gate_bias = jnp.pad(jnp.concatenate([gate_b_i[j], gate_b_f[j]]), (0, LANES - n_gate)).reshape(1, LANES)
            prm = (norm_mix_g[l], w_in[:, :4 * width].astype(BF16), w_gate, gate_bias, qk_conv_w[j], qk_conv_b[j],
                   mlstm_norm_g[j], w_out_odd[j].astype(BF16))
            xp, qkp, cpn, npn, mpn = _odd_layer(xp, *prm, None, tile=PROMPT_TILE, m_chunk=MLSTM_CHUNK)
            xs, qks, csn, nsn, msn = _odd_layer(
                xs, *prm, (state_qk_conv[j], state_mlstm_C[j], state_mlstm_n[j], state_mlstm_m[j]),
                tile=t_len, m_chunk=t_len)
            for lst, val in zip(outs_p[3:], (qkp, cpn, npn, mpn)):
                lst.append(val)
            for lst, val in zip(outs_s[3:], (qks, csn, nsn, msn)):
                lst.append(val)
        g_out = norm_out_g if l == depth - 1 else None
        w1 = w_ff1[l].astype(BF16)
        w2 = w_ff2[l].astype(BF16)
        xp = _ffn(xp.reshape(bp * s_len, d), norm_ffn_g[l], w1, w2, g_out, tile_rows=FFN_ROWS).reshape(bp, s_len, d)
        xs = _ffn(xs.reshape(bs * t_len, d), norm_ffn_g[l], w1, w2, g_out, tile_rows=FFN_ROWS).reshape(bs, t_len, d)
    return (xp, xs) + tuple(jnp.stack(v) for v in outs_p) + tuple(jnp.stack(v) for v in outs_s)
```

```python
import functools

import jax
import jax.numpy as jnp
from jax import lax
from jax.experimental import pallas as pl
from jax.experimental.pallas import tpu as pltpu

F32 = jnp.float32
BF16 = jnp.bfloat16

CHUNK = 64
A_HEADS = 8
A_KV_HEADS = 2
HEAD_DIM = 64
WINDOW = 128
ROPE_DIM = 16
ROPE_THETA = 500000.0
CONV_W = 31
M_HEADS = 4
QK_CONV_W = 4
PAST_LEN = 4096
EPS = 1e-6

LANES = 128
SUBLANES = 8
VMEM_LIMIT_BYTES = 56 * 1024 * 1024

NEG_INF = float("-inf")


def _rms(x, g):
    return x * lax.rsqrt(jnp.mean(x * x, axis=-1, keepdims=True) + EPS) * g


def _sigmoid(x):
    return 1.0 / (1.0 + jnp.exp(-x))


def _log_sigmoid(x):
    return jnp.minimum(x, 0.0) - jnp.log(1.0 + jnp.exp(-jnp.abs(x)))


def _dot(a, b):
    return jnp.dot(a, b, preferred_element_type=F32)


def _dot_nt(a, b):
    return lax.dot_general(a, b, (((1,), (1,)), ((), ())), preferred_element_type=F32)


def _dot_tn(a, b):
    return lax.dot_general(a, b, (((0,), (0,)), ((), ())), preferred_element_type=F32)


def _const_spec(shape):
    nd = len(shape)
    return pl.BlockSpec(shape, lambda *_: (0,) * nd, pipeline_mode=pl.Buffered(1))


def _ffn_body(*refs, n_hidden_chunks, hidden_chunk, final_norm):
    if final_norm:
        x_ref, g_ref, w1_ref, w2_ref, gout_ref, o_ref = refs
    else:
        x_ref, g_ref, w1_ref, w2_ref, o_ref = refs
    x = x_ref[...]
    xb = _rms(x, g_ref[...]).astype(BF16)
    acc = x
    for c in range(n_hidden_chunks):
        cols = slice(c * hidden_chunk, (c + 1) * hidden_chunk)
        h = jnp.maximum(_dot(xb, w1_ref[:, cols]), 0.0)
        acc = acc + _dot((h * h).astype(BF16), w2_ref[cols, :])
    if final_norm:
        acc = _rms(acc, gout_ref[...])
    o_ref[...] = acc


def _ffn(x, g, w1, w2, g_out, *, tile_rows):
    n, d = x.shape
    d_ff = w1.shape[1]
    hidden_chunk = min(d_ff, 1024)
    final_norm = g_out is not None
    tile_rows = min(tile_rows, n)
    assert n % tile_rows == 0 and d_ff % hidden_chunk == 0
    body = functools.partial(_ffn_body, n_hidden_chunks=d_ff // hidden_chunk, hidden_chunk=hidden_chunk,
                             final_norm=final_norm)
    in_specs = [pl.BlockSpec((tile_rows, d), lambda i: (i, 0)), _const_spec((1, d)), _const_spec((d, d_ff)),
                _const_spec((d_ff, d))]
    args = [x, g.reshape(1, d), w1, w2]
    if final_norm:
        in_specs.append(_const_spec((1, d)))
        args.append(g_out.reshape(1, d))
    return pl.pallas_call(
        body,
        grid=(n // tile_rows,),
        in_specs=in_specs,
        out_specs=pl.BlockSpec((tile_rows, d), lambda i: (i, 0)),
        out_shape=jax.ShapeDtypeStruct((n, d), F32),
        compiler_params=pltpu.CompilerParams(dimension_semantics=("arbitrary",), vmem_limit_bytes=VMEM_LIMIT_BYTES),
        name="ffn_final" if final_norm else "ffn",
    )(*args)


CONV_PAD = 32
CONV_ROWS = 64


def _even_body(*refs, tile, q_chunk, has_hist, n_tiles):
    if has_hist:
        (x_ref, g_ref, win_ref, cos_ref, sa_ref, sb_ref, sink_ref, cw_ref, cb_ref, lng_ref, lnb_ref, wout_ref,
         kh_ref, vh_ref, ch_ref, xo_ref, ko_ref, vo_ref, co_ref,
         z_ref, q_ref, kext_ref, vext_ref, kvar_ref, vvar_ref, uext_ref, mix_ref, s_ref, p_ref) = refs
    else:
        (x_ref, g_ref, win_ref, cos_ref, sa_ref, sb_ref, sink_ref, cw_ref, cb_ref, lng_ref, lnb_ref, wout_ref,
         xo_ref, ko_ref, vo_ref, co_ref,
         z_ref, q_ref, kext_ref, vext_ref, kvar_ref, vvar_ref, uext_ref, mix_ref, s_ref, p_ref) = refs
    t = pl.program_id(1)
    a_q = A_HEADS * HEAD_DIM
    a_kv = A_KV_HEADS * HEAD_DIM
    conv_ch = cw_ref.shape[1]
    n_slab = conv_ch // LANES
    n_keys = WINDOW + q_chunk
    hist_rows = CONV_W - 1

    @pl.when(t == 0)
    def _init():
        for j in range(n_slab):
            uext_ref[j, 0:CONV_PAD, :] = jnp.zeros((CONV_PAD, LANES), F32)
        if has_hist:
            kext_ref[0:WINDOW, :] = kh_ref[0]
            vext_ref[0:WINDOW, :] = vh_ref[0]
            for j in range(n_slab):
                uext_ref[j, CONV_PAD - hist_rows:CONV_PAD, :] = ch_ref[0, :, j * LANES:(j + 1) * LANES]
        else:
            kext_ref[0:WINDOW, :] = jnp.zeros((WINDOW, a_kv), F32)
            vext_ref[0:WINDOW, :] = jnp.zeros((WINDOW, a_kv), F32)

    x = x_ref[0]
    hn = _rms(x, g_ref[...]).astype(BF16)
    z_ref[...] = _dot(hn, win_ref[...])

    cos_t = cos_ref[...]
    sin_a = sa_ref[...]
    sin_b = sb_ref[...]
    half = ROPE_DIM // 2

    def rope(v):
        return v * cos_t + pltpu.roll(v, LANES - half, 1) * sin_a + pltpu.roll(v, half, 1) * sin_b

    scale = HEAD_DIM ** -0.5
    for j in range(a_q // LANES):
        cols = slice(j * LANES, (j + 1) * LANES)
        q_ref[:, cols] = (rope(z_ref[:, cols]) * scale).astype(BF16)
    kext_ref[WINDOW:WINDOW + tile, :] = rope(z_ref[:, a_q:a_q + a_kv])
    vext_ref[WINDOW:WINDOW + tile, :] = z_ref[:, a_q + a_kv:a_q + 2 * a_kv]

    lane = lax.broadcasted_iota(jnp.int32, (1, LANES), 1)
    low = lane < HEAD_DIM
    for src_ref, var_ref in ((kext_ref, kvar_ref), (vext_ref, vvar_ref)):
        kx = src_ref[...]
        kr = pltpu.roll(kx, HEAD_DIM, 1)
        var_ref[0] = jnp.where(low, kx, 0.0).astype(BF16)
        var_ref[1] = jnp.where(low, 0.0, kr).astype(BF16)
        var_ref[2] = jnp.where(low, kr, 0.0).astype(BF16)
        var_ref[3] = jnp.where(low, 0.0, kx).astype(BF16)

    group = A_HEADS // A_KV_HEADS
    row = lax.broadcasted_iota(jnp.int32, (2 * q_chunk, 1), 0)
    first_rows = row < q_chunk

    n_chunks = tile // q_chunk
    for c in range(n_chunks):
        r0 = c * q_chunk
        for g in range(A_KV_HEADS):
            qa = q_ref[r0:r0 + q_chunk, 2 * g * LANES:(2 * g + 1) * LANES]
            qb = q_ref[r0:r0 + q_chunk, (2 * g + 1) * LANES:(2 * g + 2) * LANES]
            qs = jnp.concatenate([qa, qb], axis=0)
            for hi in range(2):
                s_ref[(c * A_KV_HEADS + g) * 2 + hi] = _dot_nt(qs, kvar_ref[2 * g + hi, r0:r0 + n_keys, :])
    for c in range(n_chunks):
        r0 = c * q_chunk
        masked = (not has_hist) and r0 < WINDOW
        if masked:
            key_pos = t * tile + r0 - WINDOW + lax.broadcasted_iota(jnp.int32, (1, n_keys), 1)
            bias = jnp.where(key_pos >= 0, 0.0, NEG_INF)
        for g in range(A_KV_HEADS):
            for hi in range(2):
                i = (c * A_KV_HEADS + g) * 2 + hi
                s = s_ref[i]
                if masked:
                    s = s + bias
                sk = jnp.where(first_rows, sink_ref[group * g + hi], sink_ref[group * g + 2 + hi])
                mx = jnp.maximum(jnp.max(s, axis=-1, keepdims=True), sk)
                p = jnp.exp(s - mx)
                den = jnp.sum(p, axis=-1, keepdims=True) + jnp.exp(sk - mx)
                p_ref[i] = (p * (1.0 / den)).astype(BF16)
    for c in range(n_chunks):
        r0 = c * q_chunk
        for g in range(A_KV_HEADS):
            i = (c * A_KV_HEADS + g) * 2
            acc = (_dot(p_ref[i], vvar_ref[2 * g, r0:r0 + n_keys, :])
                   + _dot(p_ref[i + 1], vvar_ref[2 * g + 1, r0:r0 + n_keys, :]))
            mix_ref[r0:r0 + q_chunk, 2 * g * LANES:(2 * g + 1) * LANES] = acc[0:q_chunk].astype(BF16)
            mix_ref[r0:r0 + q_chunk, (2 * g + 1) * LANES:(2 * g + 2) * LANES] = acc[q_chunk:].astype(BF16)

    glu0 = a_q + 2 * a_kv
    for j in range(n_slab):
        lo = glu0 + j * LANES
        uext_ref[j, CONV_PAD:CONV_PAD + tile, :] = (
            z_ref[:, lo:lo + LANES] * _sigmoid(z_ref[:, lo + conv_ch:lo + conv_ch + LANES]))
    conv_rows = min(CONV_ROWS, tile)
    for i in range(tile // conv_rows):
        r0 = i * conv_rows
        accs = []
        for j in range(n_slab):
            cols = slice(j * LANES, (j + 1) * LANES)
            acc = jnp.zeros((conv_rows, LANES), F32)
            for w in range(CONV_W):
                acc = acc + uext_ref[j, pl.ds(r0 + CONV_PAD - hist_rows + w, conv_rows), :] * cw_ref[w:w + 1, cols]
            accs.append(acc + cb_ref[:, cols])
        mean = sum(jnp.sum(a, axis=-1, keepdims=True) for a in accs) * (1.0 / conv_ch)
        cent = [a - mean for a in accs]
        var = sum(jnp.sum(xc * xc, axis=-1, keepdims=True) for xc in cent) * (1.0 / conv_ch)
        rstd = lax.rsqrt(var + EPS)
        for j in range(n_slab):
            cols = slice(j * LANES, (j + 1) * LANES)
            y = cent[j] * rstd * lng_ref[:, cols] + lnb_ref[:, cols]
            mix_ref[r0:r0 + conv_rows, a_q + j * LANES:a_q + (j + 1) * LANES] = (y * _sigmoid(y)).astype(BF16)

    xo_ref[0] = x + _dot(mix_ref[...], wout_ref[...])

    ko_ref[0] = kext_ref[tile:tile + WINDOW, :]
    vo_ref[0] = vext_ref[tile:tile + WINDOW, :]
    for j in range(n_slab):
        co_ref[0, :, j * LANES:(j + 1) * LANES] = uext_ref[j, tile + CONV_PAD - hist_rows:tile + CONV_PAD, :]
    if n_tiles > 1:
        kext_ref[0:WINDOW, :] = kext_ref[tile:tile + WINDOW, :]
        vext_ref[0:WINDOW, :] = vext_ref[tile:tile + WINDOW, :]
        for j in range(n_slab):
            uext_ref[j, 0:CONV_PAD, :] = uext_ref[j, tile:tile + CONV_PAD, :]


def _rope_tables(pos0, t_len):
    half = ROPE_DIM // 2
    pos = pos0 + jnp.arange(t_len, dtype=jnp.int32)
    inv = ROPE_THETA ** (-jnp.arange(half, dtype=F32) / half)
    ang = pos.astype(F32)[:, None] * inv[None, :]
    cos = jnp.cos(ang)
    sin = jnp.sin(ang)
    ones = jnp.ones((t_len, HEAD_DIM - ROPE_DIM), F32)
    zeros = jnp.zeros((t_len, HEAD_DIM - ROPE_DIM), F32)
    zh = jnp.zeros((t_len, half), F32)
    reps = LANES // HEAD_DIM
    cos_t = jnp.tile(jnp.concatenate([cos, cos, ones], axis=1), (1, reps))
    sin_a = jnp.tile(jnp.concatenate([-sin, zh, zeros], axis=1), (1, reps))
    sin_b = jnp.tile(jnp.concatenate([zh, sin, zeros], axis=1), (1, reps))
    return cos_t, sin_a, sin_b


def _even_layer(x, g, w_in, sink, cw, cb, lng, lnb, w_out, hist, pos0, *, tile, q_chunk):
    bsz, t_len, d = x.shape
    tile = min(tile, t_len)
    assert t_len % tile == 0 and tile % q_chunk == 0
    n_tiles = t_len // tile
    has_hist = hist is not None
    assert has_hist or q_chunk == CHUNK
    assert (not has_hist) or n_tiles == 1
    e_in = w_in.shape[1]
    conv_ch = cw.shape[1]
    a_kv = A_KV_HEADS * HEAD_DIM
    a_q = A_HEADS * HEAD_DIM
    hist_rows = CONV_W - 1
    cos_t, sin_a, sin_b = _rope_tables(pos0, t_len)

    tab_spec = pl.BlockSpec((tile, LANES), lambda b, t: (t, 0))
    in_specs = [
        pl.BlockSpec((1, tile, d), lambda b, t: (b, t, 0)),
        _const_spec((1, d)), _const_spec((d, e_in)),
        tab_spec, tab_spec, tab_spec,
        pl.BlockSpec(memory_space=pltpu.SMEM),
        _const_spec((CONV_W, conv_ch)), _const_spec((1, conv_ch)), _const_spec((1, conv_ch)),
        _const_spec((1, conv_ch)), _const_spec((a_q + conv_ch, d)),
    ]
    args = [x, g.reshape(1, d), w_in, cos_t, sin_a, sin_b, sink, cw, cb.reshape(1, conv_ch),
            lng.reshape(1, conv_ch), lnb.reshape(1, conv_ch), w_out]
    if has_hist:
        layer, k_hist, v_hist, c_hist = hist
        in_specs += [pl.BlockSpec((1, WINDOW, a_kv), lambda b, t: (b, 0, 0)),
                     pl.BlockSpec((1, WINDOW, a_kv), lambda b, t: (b, 0, 0)),
                     pl.BlockSpec((None, 1, hist_rows, conv_ch), lambda b, t: (layer, b, 0, 0))]
        args += [k_hist[layer].reshape(bsz, WINDOW, a_kv), v_hist[layer].reshape(bsz, WINDOW, a_kv), c_hist]
    out_specs = [pl.BlockSpec((1, tile, d), lambda b, t: (b, t, 0)),
                 pl.BlockSpec((1, WINDOW, a_kv), lambda b, t: (b, 0, 0)),
                 pl.BlockSpec((1, WINDOW, a_kv), lambda b, t: (b, 0, 0)),
                 pl.BlockSpec((1, hist_rows, conv_ch), lambda b, t: (b, 0, 0))]
    out_shape = [jax.ShapeDtypeStruct((bsz, t_len, d), F32),
                 jax.ShapeDtypeStruct((bsz, WINDOW, a_kv), F32),
                 jax.ShapeDtypeStruct((bsz, WINDOW, a_kv), F32),
                 jax.ShapeDtypeStruct((bsz, hist_rows, conv_ch), F32)]
    n_blocks = 2 * A_KV_HEADS * (tile // q_chunk)
    scratch = [pltpu.VMEM((tile, e_in), F32),
               pltpu.VMEM((tile, a_q), BF16),
               pltpu.VMEM((WINDOW + tile, a_kv), F32),
               pltpu.VMEM((WINDOW + tile, a_kv), F32),
               pltpu.VMEM((4, WINDOW + tile, LANES), BF16),
               pltpu.VMEM((4, WINDOW + tile, LANES), BF16),
               pltpu.VMEM((conv_ch // LANES, CONV_PAD + tile, LANES), F32),
               pltpu.VMEM((tile, a_q + conv_ch), BF16),
               pltpu.VMEM((n_blocks, 2 * q_chunk, WINDOW + q_chunk), F32),
               pltpu.VMEM((n_blocks, 2 * q_chunk, WINDOW + q_chunk), BF16)]
    body = functools.partial(_even_body, tile=tile, q_chunk=q_chunk, has_hist=has_hist, n_tiles=n_tiles)
    xo, ko, vo, co = pl.pallas_call(
        body,
        grid=(bsz, n_tiles),
        in_specs=in_specs,
        out_specs=out_specs,
        out_shape=out_shape,
        scratch_shapes=scratch,
        compiler_params=pltpu.CompilerParams(dimension_semantics=("arbitrary", "arbitrary"),
                                             vmem_limit_bytes=VMEM_LIMIT_BYTES),
        name="even_mixer_hist" if has_hist else "even_mixer",
    )(*args)
    kv_shape = (bsz, WINDOW, A_KV_HEADS, HEAD_DIM)
    return xo, ko.reshape(kv_shape), vo.reshape(kv_shape), co


QK_PAD = 8
QK_CONV_ROWS = 128
GATE_LANE0 = M_HEADS


def _cumsum_rows(x):
    n = x.shape[0]
    tri = (lax.broadcasted_iota(jnp.int32, (n, n), 0) >= lax.broadcasted_iota(jnp.int32, (n, n), 1)).astype(BF16)
    x1 = x.astype(BF16)
    r1 = x - x1.astype(F32)
    x2 = r1.astype(BF16)
    x3 = (r1 - x2.astype(F32)).astype(BF16)
    return _dot(tri, x1) + _dot(tri, x2) + _dot(tri, x3)


def _odd_body(*refs, tile, m_chunk, has_hist, n_tiles):
    if has_hist:
        (x_ref, g_ref, win_ref, wg_ref, gb_ref, cw_ref, cb_ref, ng_ref, wout_ref,
         qkh_ref, c0_ref, n0_ref, m0_ref, xo_ref, qko_ref, co_ref, no_ref, mo_ref,
         qkext_ref, z_ref, qs_ref, ks_ref, mix_ref, c_ref, n_ref, m_ref,
         gq_ref, sqk_ref, sbf_ref, rs_ref, wi_ref, em_ref, vb_ref, vw_ref, dn_ref, num_ref, dc_ref, cst_ref, nst_ref,
         numc_ref) = refs
    else:
        (x_ref, g_ref, win_ref, wg_ref, gb_ref, cw_ref, cb_ref, ng_ref, wout_ref,
         xo_ref, qko_ref, co_ref, no_ref, mo_ref,
         qkext_ref, z_ref, qs_ref, ks_ref, mix_ref, c_ref, n_ref, m_ref,
         gq_ref, sqk_ref, sbf_ref, rs_ref, wi_ref, em_ref, vb_ref, vw_ref, dn_ref, num_ref, dc_ref, cst_ref, nst_ref,
         numc_ref) = refs
    t = pl.program_id(1)
    width = ng_ref.shape[1]
    m_dim = width // M_HEADS
    n_slab = 2 * width // LANES
    hist_rows = QK_CONV_W - 1
    length = m_chunk

    @pl.when(t == 0)
    def _init():
        for j in range(n_slab):
            qkext_ref[j, 0:QK_PAD, :] = jnp.zeros((QK_PAD, LANES), F32)
        if has_hist:
            for j in range(n_slab):
                qkext_ref[j, QK_PAD - hist_rows:QK_PAD, :] = qkh_ref[0, :, j * LANES:(j + 1) * LANES]
            c_ref[...] = c0_ref[0]
            n_ref[...] = n0_ref[0]
            m_ref[...] = m0_ref[0]
        else:
            c_ref[...] = jnp.zeros(c_ref.shape, F32)
            n_ref[...] = jnp.zeros(n_ref.shape, F32)
            m_ref[...] = jnp.zeros(m_ref.shape, F32)

    x = x_ref[0]
    hn = _rms(x, g_ref[...]).astype(BF16)
    for n in range(n_slab // 2):
        r = _dot(hn, win_ref[:, 2 * n * LANES:(2 * n + 2) * LANES])
        qkext_ref[2 * n, QK_PAD:QK_PAD + tile, :] = r[:, 0:LANES]
        qkext_ref[2 * n + 1, QK_PAD:QK_PAD + tile, :] = r[:, LANES:2 * LANES]
    z_ref[...] = _dot(hn, win_ref[:, 2 * width:4 * width])
    gts = _dot(hn, wg_ref[...]) + gb_ref[...]

    conv_rows = min(QK_CONV_ROWS, tile)
    k_scale = m_dim ** -0.5
    for j in range(n_slab):
        cols = slice(j * LANES, (j + 1) * LANES)
        is_q = j * LANES < width
        dst_ref = qs_ref if is_q else ks_ref
        dcols = slice(j * LANES % width, j * LANES % width + LANES)
        for i in range(tile // conv_rows):
            r0 = i * conv_rows
            acc = jnp.zeros((conv_rows, LANES), F32)
            for w in range(QK_CONV_W):
                acc = acc + qkext_ref[j, pl.ds(r0 + QK_PAD - hist_rows + w, conv_rows), :] * cw_ref[w:w + 1, cols]
            acc = acc + cb_ref[:, cols]
            y = acc * _sigmoid(acc)
            if not is_q:
                y = y * k_scale
            dst_ref[r0:r0 + conv_rows, dcols] = y.astype(BF16)

    lane = lax.broadcasted_iota(jnp.int32, (1, LANES), 1)
    gate_lanes = (lane >= GATE_LANE0) & (lane < GATE_LANE0 + M_HEADS)
    lf_all = jnp.where(gate_lanes, _log_sigmoid(gts), 0.0)
    ig_all = jnp.where(gate_lanes, pltpu.roll(gts, GATE_LANE0, 1), 0.0)
    causal = (lax.broadcasted_iota(jnp.int32, (length, length), 0)
              >= lax.broadcasted_iota(jnp.int32, (length, length), 1))
    pad_rows = LANES - length if length < LANES else 0
    n_ch = tile // length
    lrows = slice(0, length)


    m_run = m_ref[...]
    w_c_rows = []
    for c in range(n_ch):
        rows = slice(c * length, (c + 1) * length)
        bcum = _cumsum_rows(lf_all[rows])
        b_end = bcum[length - 1:length, :]
        r_all = ig_all[rows] - bcum
        dec = b_end + r_all
        m_new = jnp.maximum(b_end + m_run, jnp.max(dec, axis=0, keepdims=True))
        gq_ref[c, 0, lrows, :] = bcum
        gq_ref[c, 1, lrows, :] = bcum + m_run
        gq_ref[c, 2, lrows, :] = jnp.exp(dec - m_new)
        r_pad = r_all if pad_rows == 0 else jnp.concatenate([r_all, jnp.zeros((pad_rows, LANES), F32)], axis=0)
        gq_ref[c, 3] = r_pad.T
        w_c_rows.append(jnp.exp(b_end + m_run - m_new))
        m_run = jnp.where(gate_lanes, m_new, 0.0)
    m_ref[...] = m_run

    def pieces():
        for c in range(n_ch):
            for h in range(M_HEADS):
                yield (c, h, c * M_HEADS + h, slice(c * length, (c + 1) * length), slice(h * m_dim, (h + 1) * m_dim),
                       GATE_LANE0 + h)

    for c, h, i, rows, hcols, gl in pieces():
        sqk_ref[i] = _dot_nt(qs_ref[rows, hcols], ks_ref[rows, hcols])

    for c, h, i, rows, hcols, gl in pieces():
        a_h = gq_ref[c, 1, lrows, gl:gl + 1]
        dmat = jnp.where(causal, gq_ref[c, 0, lrows, gl:gl + 1] + gq_ref[c, 3, gl:gl + 1, lrows], NEG_INF)
        m_t = jnp.maximum(a_h, jnp.max(dmat, axis=-1, keepdims=True))
        s = sqk_ref[i] * jnp.exp(dmat - m_t)
        sbf_ref[i] = s.astype(BF16)
        rs_ref[i] = jnp.broadcast_to(jnp.sum(s, axis=-1, keepdims=True), (length, LANES))
        wi_ref[i] = jnp.broadcast_to(jnp.exp(a_h - m_t), (length, LANES))
        em_ref[i] = jnp.broadcast_to(jnp.exp(-m_t), (length, LANES))
        w_s = gq_ref[c, 2, lrows, gl:gl + 1]
        vh = z_ref[rows, hcols]
        vb_ref[rows, hcols] = vh.astype(BF16)
        vw_ref[rows, hcols] = (vh * w_s).astype(BF16)
        dn_ref[i:i + 1, :] = jnp.sum(ks_ref[rows, hcols].astype(F32) * w_s, axis=0, keepdims=True)

    for c, h, i, rows, hcols, gl in pieces():
        num_ref[rows, hcols] = _dot(sbf_ref[i], vb_ref[rows, hcols])
        dc_ref[i] = _dot_tn(vw_ref[rows, hcols], ks_ref[rows, hcols])

    for c, h, i, rows, hcols, gl in pieces():
        w_c = w_c_rows[c][:, gl:gl + 1]
        c_h = c_ref[h]
        cst_ref[i] = c_h.astype(BF16)
        c_ref[h] = w_c * c_h + dc_ref[i]
        n_h = n_ref[h:h + 1, :]
        nst_ref[i:i + 1, :] = n_h
        n_ref[h:h + 1, :] = w_c * n_h + dn_ref[i:i + 1, :]

    for c, h, i, rows, hcols, gl in pieces():
        numc_ref[rows, hcols] = _dot_nt(qs_ref[rows, hcols], cst_ref[i])

    half_cols = m_dim // LANES
    for c, h, i, rows, hcols, gl in pieces():
        w_inter = wi_ref[i]
        qn = jnp.sum(qs_ref[rows, hcols].astype(F32) * nst_ref[i:i + 1, :], axis=-1, keepdims=True)
        den = rs_ref[i] + w_inter * qn
        inv = 1.0 / jnp.maximum(jnp.abs(den), em_ref[i])
        hh = []
        for j in range(half_cols):
            cols = slice(h * m_dim + j * LANES, h * m_dim + (j + 1) * LANES)
            hh.append((num_ref[rows, cols] + w_inter * numc_ref[rows, cols]) * inv)
        ms = sum(jnp.sum(v * v, axis=-1, keepdims=True) for v in hh) * (1.0 / m_dim)
        rstd = lax.rsqrt(ms + EPS)
        for j in range(half_cols):
            cols = slice(h * m_dim + j * LANES, h * m_dim + (j + 1) * LANES)
            og = _sigmoid(z_ref[rows, width + h * m_dim + j * LANES:width + h * m_dim + (j + 1) * LANES])
            mix_ref[rows, cols] = (hh[j] * rstd * ng_ref[:, cols] * og).astype(BF16)

    xo_ref[0] = x + _dot(mix_ref[...], wout_ref[...])

    for j in range(n_slab):
        qko_ref[0, :, j * LANES:(j + 1) * LANES] = qkext_ref[j, tile + QK_PAD - hist_rows:tile + QK_PAD, :]
    co_ref[0] = c_ref[...]
    no_ref[0] = n_ref[...]
    mo_ref[0] = m_ref[...]
    if n_tiles > 1:
        for j in range(n_slab):
            qkext_ref[j, 0:QK_PAD, :] = qkext_ref[j, tile:tile + QK_PAD, :]


def _odd_layer(x, g, w_in, w_gate, gate_bias, cw, cb, ng, w_out, hist, *, tile, m_chunk):
    bsz, t_len, d = x.shape
    tile = min(tile, t_len)
    m_chunk = min(m_chunk, tile)
    assert t_len % tile == 0 and tile % m_chunk == 0
    n_tiles = t_len // tile
    has_hist = hist is not None
    assert (not has_hist) or n_tiles == 1
    width = ng.shape[0]
    m_dim = width // M_HEADS
    hist_rows = QK_CONV_W - 1
    n_ch = tile // m_chunk
    n_hc = n_ch * M_HEADS

    in_specs = [
        pl.BlockSpec((1, tile, d), lambda b, t: (b, t, 0)),
        _const_spec((1, d)), _const_spec((d, 4 * width)), _const_spec((d, LANES)), _const_spec((1, LANES)),
        _const_spec((QK_CONV_W, 2 * width)), _const_spec((1, 2 * width)), _const_spec((1, width)),
        _const_spec((width, d)),
    ]
    args = [x, g.reshape(1, d), w_in, w_gate, gate_bias, cw, cb.reshape(1, 2 * width), ng.reshape(1, width), w_out]
    if has_hist:
        layer, qk_hist, c0, n0, m0 = hist
        m0_lanes = jnp.pad(m0[layer], ((0, 0), (GATE_LANE0, LANES - GATE_LANE0 - M_HEADS))).reshape(bsz, 1, LANES)
        in_specs += [pl.BlockSpec((None, 1, hist_rows, 2 * width), lambda b, t: (layer, b, 0, 0)),
                     pl.BlockSpec((None, 1, M_HEADS, m_dim, m_dim), lambda b, t: (layer, b, 0, 0, 0)),
                     pl.BlockSpec((None, 1, M_HEADS, m_dim), lambda b, t: (layer, b, 0, 0)),
                     pl.BlockSpec((1, 1, LANES), lambda b, t: (b, 0, 0))]
        args += [qk_hist, c0, n0, m0_lanes]
    out_specs = [pl.BlockSpec((1, tile, d), lambda b, t: (b, t, 0)),
                 pl.BlockSpec((1, hist_rows, 2 * width), lambda b, t: (b, 0, 0)),
                 pl.BlockSpec((1, M_HEADS, m_dim, m_dim), lambda b, t: (b, 0, 0, 0)),
                 pl.BlockSpec((1, M_HEADS, m_dim), lambda b, t: (b, 0, 0)),
                 pl.BlockSpec((1, 1, LANES), lambda b, t: (b, 0, 0))]
    out_shape = [jax.ShapeDtypeStruct((bsz, t_len, d), F32),
                 jax.ShapeDtypeStruct((bsz, hist_rows, 2 * width), F32),
                 jax.ShapeDtypeStruct((bsz, M_HEADS, m_dim, m_dim), F32),
                 jax.ShapeDtypeStruct((bsz, M_HEADS, m_dim), F32),
                 jax.ShapeDtypeStruct((bsz, 1, LANES), F32)]
    scratch = [pltpu.VMEM((2 * width // LANES, QK_PAD + tile, LANES), F32),
               pltpu.VMEM((tile, 2 * width), F32),
               pltpu.VMEM((tile, width), BF16),
               pltpu.VMEM((tile, width), BF16),
               pltpu.VMEM((tile, width), BF16),
               pltpu.VMEM((M_HEADS, m_dim, m_dim), F32),
               pltpu.VMEM((M_HEADS, m_dim), F32),
               pltpu.VMEM((1, LANES), F32),
               pltpu.VMEM((n_ch, 4, LANES, LANES), F32),
               pltpu.VMEM((n_hc, m_chunk, m_chunk), F32),
               pltpu.VMEM((n_hc, m_chunk, m_chunk), BF16),
               pltpu.VMEM((n_hc, m_chunk, LANES), F32),
               pltpu.VMEM((n_hc, m_chunk, LANES), F32),
               pltpu.VMEM((n_hc, m_chunk, LANES), F32),
               pltpu.VMEM((tile, width), BF16),
               pltpu.VMEM((tile, width), BF16),
               pltpu.VMEM((n_hc, m_dim), F32),
               pltpu.VMEM((tile, width), F32),
               pltpu.VMEM((n_hc, m_dim, m_dim), F32),
               pltpu.VMEM((n_hc, m_dim, m_dim), BF16),
               pltpu.VMEM((n_hc, m_dim), F32),
               pltpu.VMEM((tile, width), F32)]
    body = functools.partial(_odd_body, tile=tile, m_chunk=m_chunk, has_hist=has_hist, n_tiles=n_tiles)
    xo, qko, co, no, mo = pl.pallas_call(
        body,
        grid=(bsz, n_tiles),
        in_specs=in_specs,
        out_specs=out_specs,
        out_shape=out_shape,
        scratch_shapes=scratch,
        compiler_params=pltpu.CompilerParams(dimension_semantics=("arbitrary", "arbitrary"),
                                             vmem_limit_bytes=VMEM_LIMIT_BYTES),
        name="mlstm_mixer_hist" if has_hist else "mlstm_mixer",
    )(*args)
    return xo, qko, co, no, mo[:, 0, GATE_LANE0:GATE_LANE0 + M_HEADS]


PROMPT_TILE = 512
MLSTM_CHUNK = 128
FFN_ROWS = 512


def kernel(x_prompt, x_sample, cache_swa_k, cache_swa_v, state_conv, state_qk_conv, state_mlstm_C, state_mlstm_n,
           state_mlstm_m, norm_mix_g, norm_ffn_g, norm_out_g, w_in_even, attn_sink, conv_w, conv_b, conv_ln_g,
           conv_ln_b, w_out_even, w_in_odd, qk_conv_w, qk_conv_b, gate_b_i, gate_b_f, mlstm_norm_g, w_out_odd,
           w_ff1, w_ff2):
    depth = norm_mix_g.shape[0]
    bp, s_len, d = x_prompt.shape
    bs, t_len, _ = x_sample.shape
    width = mlstm_norm_g.shape[1]
    xp, xs = x_prompt, x_sample
    outs_p = [[] for _ in range(7)]
    outs_s = [[] for _ in range(7)]
    for l in range(depth):
        if l % 2 == 0:
            i = l // 2
            prm = (norm_mix_g[l], w_in_even[i].astype(BF16), attn_sink[i], conv_w[i], conv_b[i], conv_ln_g[i],
                   conv_ln_b[i], w_out_even[i].astype(BF16))
            xp, kp, vp, cvp = _even_layer(xp, *prm, None, 0, tile=PROMPT_TILE, q_chunk=CHUNK)
            xs, ks_, vs_, cvs = _even_layer(xs, *prm, (i, cache_swa_k, cache_swa_v, state_conv), PAST_LEN,
                                            tile=t_len, q_chunk=t_len)
            for lst, val in zip(outs_p[:3], (kp, vp, cvp)):
                lst.append(val)
            for lst, val in zip(outs_s[:3], (ks_, vs_, cvs)):
                lst.append(val)
        else:
            j = l // 2
            w_in = w_in_odd[j]
            n_gate = 2 * M_HEADS
            w_gate = jnp.pad(w_in[:, 4 * width:], ((0, 0), (0, LANES - n_gate))).astype(BF16)
            gate_bias = jnp.pad(jnp.concatenate([gate_b_i[j], gate_b_f[j]]), (0, LANES - n_gate)).reshape(1, LANES)
            prm = (norm_mix_g[l], w_in[:, :4 * width].astype(BF16), w_gate, gate_bias, qk_conv_w[j], qk_conv_b[j],
                   mlstm_norm_g[j], w_out_odd[j].astype(BF16))
            xp, qkp, cpn, npn, mpn = _odd_layer(xp, *prm, None, tile=PROMPT_TILE, m_chunk=MLSTM_CHUNK)
            xs, qks, csn, nsn, msn = _odd_layer(
                xs, *prm, (j, state_qk_conv, state_mlstm_C, state_mlstm_n, state_mlstm_m),
                tile=t_len, m_chunk=t_len)
            for lst, val in zip(outs_p[3:], (qkp, cpn, npn, mpn)):
                lst.append(val)
            for lst, val in zip(outs_s[3:], (qks, csn, nsn, msn)):
                lst.append(val)
        g_out = norm_out_g if l == depth - 1 else None
        w1 = w_ff1[l].astype(BF16)
        w2 = w_ff2[l].astype(BF16)
        xp = _ffn(xp.reshape(bp * s_len, d), norm_ffn_g[l], w1, w2, g_out, tile_rows=FFN_ROWS).reshape(bp, s_len, d)
        xs = _ffn(xs.reshape(bs * t_len, d), norm_ffn_g[l], w1, w2, g_out, tile_rows=FFN_ROWS).reshape(bs, t_len, d)
    return (xp, xs) + tuple(jnp.stack(v) for v in outs_p) + tuple(jnp.stack(v) for v in outs_s)
```

```python
import functools

import jax
import jax.numpy as jnp
from jax import lax
from jax.experimental import pallas as pl
from jax.experimental.pallas import tpu as pltpu

F32 = jnp.float32
BF16 = jnp.bfloat16

CHUNK = 64
A_HEADS = 8
A_KV_HEADS = 2
HEAD_DIM = 64
WINDOW = 128
ROPE_DIM = 16
ROPE_THETA = 500000.0
CONV_W = 31
M_HEADS = 4
QK_CONV_W = 4
PAST_LEN = 4096
EPS = 1e-6

LANES = 128
SUBLANES = 8
VMEM_LIMIT_BYTES = 56 * 1024 * 1024

NEG_INF = float("-inf")


def _rms(x, g):
    return x * lax.rsqrt(jnp.mean(x * x, axis=-1, keepdims=True) + EPS) * g


def _sigmoid(x):
    return 0.5 * jnp.tanh(0.5 * x) + 0.5


def _log_sigmoid(x):
    return jnp.minimum(x, 0.0) - jnp.log(1.0 + jnp.exp(-jnp.abs(x)))


def _dot(a, b):
    return jnp.dot(a, b, preferred_element_type=F32)


def _dot_nt(a, b):
    return lax.dot_general(a, b, (((1,), (1,)), ((), ())), preferred_element_type=F32)


def _dot_tn(a, b):
    return lax.dot_general(a, b, (((0,), (0,)), ((), ())), preferred_element_type=F32)


def _const_spec(shape):
    nd = len(shape)
    return pl.BlockSpec(shape, lambda *_: (0,) * nd, pipeline_mode=pl.Buffered(1))


def _ffn_body(*refs, n_hidden_chunks, hidden_chunk, final_norm):
    if final_norm:
        x_ref, g_ref, w1_ref, w2_ref, gout_ref, o_ref = refs
    else:
        x_ref, g_ref, w1_ref, w2_ref, o_ref = refs
    x = x_ref[...]
    xb = _rms(x, g_ref[...]).astype(BF16)
    acc = x
    for c in range(n_hidden_chunks):
        cols = slice(c * hidden_chunk, (c + 1) * hidden_chunk)
        h = jnp.maximum(_dot(xb, w1_ref[:, cols]), 0.0)
        acc = acc + _dot((h * h).astype(BF16), w2_ref[cols, :])
    if final_norm:
        acc = _rms(acc, gout_ref[...])
    o_ref[...] = acc


def _ffn(x, g, w1, w2, g_out, *, tile_rows):
    n, d = x.shape
    d_ff = w1.shape[1]
    hidden_chunk = min(d_ff, 1024)
    final_norm = g_out is not None
    tile_rows = min(tile_rows, n)
    assert n % tile_rows == 0 and d_ff % hidden_chunk == 0
    body = functools.partial(_ffn_body, n_hidden_chunks=d_ff // hidden_chunk, hidden_chunk=hidden_chunk,
                             final_norm=final_norm)
    in_specs = [pl.BlockSpec((tile_rows, d), lambda i: (i, 0)), _const_spec((1, d)), _const_spec((d, d_ff)),
                _const_spec((d_ff, d))]
    args = [x, g.reshape(1, d), w1, w2]
    if final_norm:
        in_specs.append(_const_spec((1, d)))
        args.append(g_out.reshape(1, d))
    return pl.pallas_call(
        body,
        grid=(n // tile_rows,),
        in_specs=in_specs,
        out_specs=pl.BlockSpec((tile_rows, d), lambda i: (i, 0)),
        out_shape=jax.ShapeDtypeStruct((n, d), F32),
        compiler_params=pltpu.CompilerParams(dimension_semantics=("arbitrary",), vmem_limit_bytes=VMEM_LIMIT_BYTES),
        name="ffn_final" if final_norm else "ffn",
    )(*args)


CONV_PAD = 32
CONV_ROWS = 64


def _even_body(*refs, tile, q_chunk, has_hist, n_tiles):
    if has_hist:
        (x_ref, g_ref, win_ref, cos_ref, sa_ref, sb_ref, sink_ref, cw_ref, cb_ref, lng_ref, lnb_ref, wout_ref,
         kh_ref, vh_ref, ch_ref, xo_ref, ko_ref, vo_ref, co_ref,
         z_ref, q_ref, kext_ref, vext_ref, kvar_ref, vvar_ref, uext_ref, mix_ref, s_ref, p_ref) = refs
    else:
        (x_ref, g_ref, win_ref, cos_ref, sa_ref, sb_ref, sink_ref, cw_ref, cb_ref, lng_ref, lnb_ref, wout_ref,
         xo_ref, ko_ref, vo_ref, co_ref,
         z_ref, q_ref, kext_ref, vext_ref, kvar_ref, vvar_ref, uext_ref, mix_ref, s_ref, p_ref) = refs
    t = pl.program_id(1)
    a_q = A_HEADS * HEAD_DIM
    a_kv = A_KV_HEADS * HEAD_DIM
    conv_ch = cw_ref.shape[1]
    n_slab = conv_ch // LANES
    n_keys = WINDOW + q_chunk
    hist_rows = CONV_W - 1

    @pl.when(t == 0)
    def _init():
        for j in range(n_slab):
            uext_ref[j, 0:CONV_PAD, :] = jnp.zeros((CONV_PAD, LANES), F32)
        if has_hist:
            kext_ref[0:WINDOW, :] = kh_ref[0]
            vext_ref[0:WINDOW, :] = vh_ref[0]
            for j in range(n_slab):
                uext_ref[j, CONV_PAD - hist_rows:CONV_PAD, :] = ch_ref[0, :, j * LANES:(j + 1) * LANES]
        else:
            kext_ref[0:WINDOW, :] = jnp.zeros((WINDOW, a_kv), F32)
            vext_ref[0:WINDOW, :] = jnp.zeros((WINDOW, a_kv), F32)

    x = x_ref[0]
    hn = _rms(x, g_ref[...]).astype(BF16)
    z_ref[...] = _dot(hn, win_ref[...])

    cos_t = cos_ref[...]
    sin_a = sa_ref[...]
    sin_b = sb_ref[...]
    half = ROPE_DIM // 2

    def rope(v):
        return v * cos_t + pltpu.roll(v, LANES - half, 1) * sin_a + pltpu.roll(v, half, 1) * sin_b

    scale = HEAD_DIM ** -0.5
    for j in range(a_q // LANES):
        cols = slice(j * LANES, (j + 1) * LANES)
        q_ref[:, cols] = (rope(z_ref[:, cols]) * scale).astype(BF16)
    kext_ref[WINDOW:WINDOW + tile, :] = rope(z_ref[:, a_q:a_q + a_kv])
    vext_ref[WINDOW:WINDOW + tile, :] = z_ref[:, a_q + a_kv:a_q + 2 * a_kv]

    lane = lax.broadcasted_iota(jnp.int32, (1, LANES), 1)
    low = lane < HEAD_DIM
    for src_ref, var_ref in ((kext_ref, kvar_ref), (vext_ref, vvar_ref)):
        kx = src_ref[...]
        kr = pltpu.roll(kx, HEAD_DIM, 1)
        var_ref[0] = jnp.where(low, kx, 0.0).astype(BF16)
        var_ref[1] = jnp.where(low, 0.0, kr).astype(BF16)
        var_ref[2] = jnp.where(low, kr, 0.0).astype(BF16)
        var_ref[3] = jnp.where(low, 0.0, kx).astype(BF16)

    group = A_HEADS // A_KV_HEADS
    row = lax.broadcasted_iota(jnp.int32, (2 * q_chunk, 1), 0)
    first_rows = row < q_chunk

    n_chunks = tile // q_chunk
    for c in range(n_chunks):
        r0 = c * q_chunk
        for g in range(A_KV_HEADS):
            qa = q_ref[r0:r0 + q_chunk, 2 * g * LANES:(2 * g + 1) * LANES]
            qb = q_ref[r0:r0 + q_chunk, (2 * g + 1) * LANES:(2 * g + 2) * LANES]
            qs = jnp.concatenate([qa, qb], axis=0)
            for hi in range(2):
                s_ref[(c * A_KV_HEADS + g) * 2 + hi] = _dot_nt(qs, kvar_ref[2 * g + hi, r0:r0 + n_keys, :])
    for c in range(n_chunks):
        r0 = c * q_chunk
        masked = (not has_hist) and r0 < WINDOW
        if masked:
            key_pos = t * tile + r0 - WINDOW + lax.broadcasted_iota(jnp.int32, (1, n_keys), 1)
            bias = jnp.where(key_pos >= 0, 0.0, NEG_INF)
        for g in range(A_KV_HEADS):
            for hi in range(2):
                i = (c * A_KV_HEADS + g) * 2 + hi
                s = s_ref[i]
                if masked:
                    s = s + bias
                sk = jnp.where(first_rows, sink_ref[group * g + hi], sink_ref[group * g + 2 + hi])
                mx = jnp.maximum(jnp.max(s, axis=-1, keepdims=True), sk)
                p = jnp.exp(s - mx)
                den = jnp.sum(p, axis=-1, keepdims=True) + jnp.exp(sk - mx)
                p_ref[i] = (p * (1.0 / den)).astype(BF16)
    for c in range(n_chunks):
        r0 = c * q_chunk
        for g in range(A_KV_HEADS):
            i = (c * A_KV_HEADS + g) * 2
            acc = (_dot(p_ref[i], vvar_ref[2 * g, r0:r0 + n_keys, :])
                   + _dot(p_ref[i + 1], vvar_ref[2 * g + 1, r0:r0 + n_keys, :]))
            mix_ref[r0:r0 + q_chunk, 2 * g * LANES:(2 * g + 1) * LANES] = acc[0:q_chunk].astype(BF16)
            mix_ref[r0:r0 + q_chunk, (2 * g + 1) * LANES:(2 * g + 2) * LANES] = acc[q_chunk:].astype(BF16)

    glu0 = a_q + 2 * a_kv
    for j in range(n_slab):
        lo = glu0 + j * LANES
        uext_ref[j, CONV_PAD:CONV_PAD + tile, :] = (
            z_ref[:, lo:lo + LANES] * _sigmoid(z_ref[:, lo + conv_ch:lo + conv_ch + LANES]))
    conv_rows = min(CONV_ROWS, tile)
    for i in range(tile // conv_rows):
        r0 = i * conv_rows
        accs = []
        for j in range(n_slab):
            cols = slice(j * LANES, (j + 1) * LANES)
            acc = jnp.zeros((conv_rows, LANES), F32)
            for w in range(CONV_W):
                acc = acc + uext_ref[j, pl.ds(r0 + CONV_PAD - hist_rows + w, conv_rows), :] * cw_ref[w:w + 1, cols]
            accs.append(acc + cb_ref[:, cols])
        mean = sum(jnp.sum(a, axis=-1, keepdims=True) for a in accs) * (1.0 / conv_ch)
        cent = [a - mean for a in accs]
        var = sum(jnp.sum(xc * xc, axis=-1, keepdims=True) for xc in cent) * (1.0 / conv_ch)
        rstd = lax.rsqrt(var + EPS)
        for j in range(n_slab):
            cols = slice(j * LANES, (j + 1) * LANES)
            y = cent[j] * rstd * lng_ref[:, cols] + lnb_ref[:, cols]
            mix_ref[r0:r0 + conv_rows, a_q + j * LANES:a_q + (j + 1) * LANES] = (y * _sigmoid(y)).astype(BF16)

    xo_ref[0] = x + _dot(mix_ref[...], wout_ref[...])

    ko_ref[0] = kext_ref[tile:tile + WINDOW, :]
    vo_ref[0] = vext_ref[tile:tile + WINDOW, :]
    for j in range(n_slab):
        co_ref[0, :, j * LANES:(j + 1) * LANES] = uext_ref[j, tile + CONV_PAD - hist_rows:tile + CONV_PAD, :]
    if n_tiles > 1:
        kext_ref[0:WINDOW, :] = kext_ref[tile:tile + WINDOW, :]
        vext_ref[0:WINDOW, :] = vext_ref[tile:tile + WINDOW, :]
        for j in range(n_slab):
            uext_ref[j, 0:CONV_PAD, :] = uext_ref[j, tile:tile + CONV_PAD, :]


def _rope_tables(pos0, t_len):
    half = ROPE_DIM // 2
    pos = pos0 + jnp.arange(t_len, dtype=jnp.int32)
    inv = ROPE_THETA ** (-jnp.arange(half, dtype=F32) / half)
    ang = pos.astype(F32)[:, None] * inv[None, :]
    cos = jnp.cos(ang)
    sin = jnp.sin(ang)
    ones = jnp.ones((t_len, HEAD_DIM - ROPE_DIM), F32)
    zeros = jnp.zeros((t_len, HEAD_DIM - ROPE_DIM), F32)
    zh = jnp.zeros((t_len, half), F32)
    reps = LANES // HEAD_DIM
    cos_t = jnp.tile(jnp.concatenate([cos, cos, ones], axis=1), (1, reps))
    sin_a = jnp.tile(jnp.concatenate([-sin, zh, zeros], axis=1), (1, reps))
    sin_b = jnp.tile(jnp.concatenate([zh, sin, zeros], axis=1), (1, reps))
    return cos_t, sin_a, sin_b


def _even_layer(x, g, w_in, sink, cw, cb, lng, lnb, w_out, hist, pos0, *, tile, q_chunk):
    bsz, t_len, d = x.shape
    tile = min(tile, t_len)
    assert t_len % tile == 0 and tile % q_chunk == 0
    n_tiles = t_len // tile
    has_hist = hist is not None
    assert has_hist or q_chunk == CHUNK
    assert (not has_hist) or n_tiles == 1
    e_in = w_in.shape[1]
    conv_ch = cw.shape[1]
    a_kv = A_KV_HEADS * HEAD_DIM
    a_q = A_HEADS * HEAD_DIM
    hist_rows = CONV_W - 1
    cos_t, sin_a, sin_b = _rope_tables(pos0, t_len)

    tab_spec = pl.BlockSpec((tile, LANES), lambda b, t: (t, 0))
    in_specs = [
        pl.BlockSpec((1, tile, d), lambda b, t: (b, t, 0)),
        _const_spec((1, d)), _const_spec((d, e_in)),
        tab_spec, tab_spec, tab_spec,
        pl.BlockSpec(memory_space=pltpu.SMEM),
        _const_spec((CONV_W, conv_ch)), _const_spec((1, conv_ch)), _const_spec((1, conv_ch)),
        _const_spec((1, conv_ch)), _const_spec((a_q + conv_ch, d)),
    ]
    args = [x, g.reshape(1, d), w_in, cos_t, sin_a, sin_b, sink, cw, cb.reshape(1, conv_ch),
            lng.reshape(1, conv_ch), lnb.reshape(1, conv_ch), w_out]
    if has_hist:
        layer, k_hist, v_hist, c_hist = hist
        in_specs += [pl.BlockSpec((1, WINDOW, a_kv), lambda b, t: (b, 0, 0)),
                     pl.BlockSpec((1, WINDOW, a_kv), lambda b, t: (b, 0, 0)),
                     pl.BlockSpec((None, 1, hist_rows, conv_ch), lambda b, t: (layer, b, 0, 0))]
        args += [k_hist[layer].reshape(bsz, WINDOW, a_kv), v_hist[layer].reshape(bsz, WINDOW, a_kv), c_hist]
    out_specs = [pl.BlockSpec((1, tile, d), lambda b, t: (b, t, 0)),
                 pl.BlockSpec((1, WINDOW, a_kv), lambda b, t: (b, 0, 0)),
                 pl.BlockSpec((1, WINDOW, a_kv), lambda b, t: (b, 0, 0)),
                 pl.BlockSpec((1, hist_rows, conv_ch), lambda b, t: (b, 0, 0))]
    out_shape = [jax.ShapeDtypeStruct((bsz, t_len, d), F32),
                 jax.ShapeDtypeStruct((bsz, WINDOW, a_kv), F32),
                 jax.ShapeDtypeStruct((bsz, WINDOW, a_kv), F32),
                 jax.ShapeDtypeStruct((bsz, hist_rows, conv_ch), F32)]
    n_blocks = 2 * A_KV_HEADS * (tile // q_chunk)
    scratch = [pltpu.VMEM((tile, e_in), F32),
               pltpu.VMEM((tile, a_q), BF16),
               pltpu.VMEM((WINDOW + tile, a_kv), F32),
               pltpu.VMEM((WINDOW + tile, a_kv), F32),
               pltpu.VMEM((4, WINDOW + tile, LANES), BF16),
               pltpu.VMEM((4, WINDOW + tile, LANES), BF16),
               pltpu.VMEM((conv_ch // LANES, CONV_PAD + tile, LANES), F32),
               pltpu.VMEM((tile, a_q + conv_ch), BF16),
               pltpu.VMEM((n_blocks, 2 * q_chunk, WINDOW + q_chunk), F32),
               pltpu.VMEM((n_blocks, 2 * q_chunk, WINDOW + q_chunk), BF16)]
    body = functools.partial(_even_body, tile=tile, q_chunk=q_chunk, has_hist=has_hist, n_tiles=n_tiles)
    xo, ko, vo, co = pl.pallas_call(
        body,
        grid=(bsz, n_tiles),
        in_specs=in_specs,
        out_specs=out_specs,
        out_shape=out_shape,
        scratch_shapes=scratch,
        compiler_params=pltpu.CompilerParams(dimension_semantics=("arbitrary", "arbitrary"),
                                             vmem_limit_bytes=VMEM_LIMIT_BYTES),
        name="even_mixer_hist" if has_hist else "even_mixer",
    )(*args)
    kv_shape = (bsz, WINDOW, A_KV_HEADS, HEAD_DIM)
    return xo, ko.reshape(kv_shape), vo.reshape(kv_shape), co


QK_PAD = 8
QK_CONV_ROWS = 128
GATE_LANE0 = M_HEADS


def _cumsum_rows(x):
    n = x.shape[0]
    tri = (lax.broadcasted_iota(jnp.int32, (n, n), 0) >= lax.broadcasted_iota(jnp.int32, (n, n), 1)).astype(BF16)
    x1 = x.astype(BF16)
    r1 = x - x1.astype(F32)
    x2 = r1.astype(BF16)
    x3 = (r1 - x2.astype(F32)).astype(BF16)
    return _dot(tri, x1) + _dot(tri, x2) + _dot(tri, x3)


def _odd_body(*refs, tile, m_chunk, has_hist, n_tiles):
    if has_hist:
        (x_ref, g_ref, win_ref, wg_ref, gb_ref, cw_ref, cb_ref, ng_ref, wout_ref,
         qkh_ref, c0_ref, n0_ref, m0_ref, xo_ref, qko_ref, co_ref, no_ref, mo_ref,
         qkext_ref, z_ref, qs_ref, ks_ref, mix_ref, c_ref, n_ref, m_ref,
         gq_ref, rt_ref, sqk_ref, sbf_ref, rs_ref, wi_ref, em_ref, vb_ref, vw_ref, dn_ref, num_ref, dc_ref, cst_ref, nst_ref,
         numc_ref) = refs
    else:
        (x_ref, g_ref, win_ref, wg_ref, gb_ref, cw_ref, cb_ref, ng_ref, wout_ref,
         xo_ref, qko_ref, co_ref, no_ref, mo_ref,
         qkext_ref, z_ref, qs_ref, ks_ref, mix_ref, c_ref, n_ref, m_ref,
         gq_ref, rt_ref, sqk_ref, sbf_ref, rs_ref, wi_ref, em_ref, vb_ref, vw_ref, dn_ref, num_ref, dc_ref, cst_ref, nst_ref,
         numc_ref) = refs
    t = pl.program_id(1)
    width = ng_ref.shape[1]
    m_dim = width // M_HEADS
    n_slab = 2 * width // LANES
    hist_rows = QK_CONV_W - 1
    length = m_chunk

    @pl.when(t == 0)
    def _init():
        for j in range(n_slab):
            qkext_ref[j, 0:QK_PAD, :] = jnp.zeros((QK_PAD, LANES), F32)
        if has_hist:
            for j in range(n_slab):
                qkext_ref[j, QK_PAD - hist_rows:QK_PAD, :] = qkh_ref[0, :, j * LANES:(j + 1) * LANES]
            c_ref[...] = c0_ref[0]
            n_ref[...] = n0_ref[0]
            m_ref[...] = m0_ref[0]
        else:
            c_ref[...] = jnp.zeros(c_ref.shape, F32)
            n_ref[...] = jnp.zeros(n_ref.shape, F32)
            m_ref[...] = jnp.zeros(m_ref.shape, F32)

    x = x_ref[0]
    hn = _rms(x, g_ref[...]).astype(BF16)
    for n in range(n_slab // 2):
        r = _dot(hn, win_ref[:, 2 * n * LANES:(2 * n + 2) * LANES])
        qkext_ref[2 * n, QK_PAD:QK_PAD + tile, :] = r[:, 0:LANES]
        qkext_ref[2 * n + 1, QK_PAD:QK_PAD + tile, :] = r[:, LANES:2 * LANES]
    z_ref[...] = _dot(hn, win_ref[:, 2 * width:4 * width])
    gts = _dot(hn, wg_ref[...]) + gb_ref[...]

    conv_rows = min(QK_CONV_ROWS, tile)
    k_scale = m_dim ** -0.5
    for j in range(n_slab):
        cols = slice(j * LANES, (j + 1) * LANES)
        is_q = j * LANES < width
        dst_ref = qs_ref if is_q else ks_ref
        dcols = slice(j * LANES % width, j * LANES % width + LANES)
        for i in range(tile // conv_rows):
            r0 = i * conv_rows
            acc = jnp.zeros((conv_rows, LANES), F32)
            for w in range(QK_CONV_W):
                acc = acc + qkext_ref[j, pl.ds(r0 + QK_PAD - hist_rows + w, conv_rows), :] * cw_ref[w:w + 1, cols]
            acc = acc + cb_ref[:, cols]
            y = acc * _sigmoid(acc)
            if not is_q:
                y = y * k_scale
            dst_ref[r0:r0 + conv_rows, dcols] = y.astype(BF16)

    lane = lax.broadcasted_iota(jnp.int32, (1, LANES), 1)
    gate_lanes = (lane >= GATE_LANE0) & (lane < GATE_LANE0 + M_HEADS)
    lf_all = jnp.where(gate_lanes, _log_sigmoid(gts), 0.0)
    ig_all = jnp.where(gate_lanes, pltpu.roll(gts, GATE_LANE0, 1), 0.0)
    causal = (lax.broadcasted_iota(jnp.int32, (length, length), 0)
              >= lax.broadcasted_iota(jnp.int32, (length, length), 1))
    pad_rows = LANES - length if length < LANES else 0
    n_ch = tile // length
    lrows = slice(0, length)


    m_run = m_ref[...]
    w_c_rows = []
    for c in range(n_ch):
        rows = slice(c * length, (c + 1) * length)
        bcum = _cumsum_rows(lf_all[rows])
        b_end = bcum[length - 1:length, :]
        r_all = ig_all[rows] - bcum
        dec = b_end + r_all
        m_new = jnp.maximum(b_end + m_run, jnp.max(dec, axis=0, keepdims=True))
        gq_ref[c, 0, lrows, :] = bcum
        gq_ref[c, 1, lrows, :] = bcum + m_run
        gq_ref[c, 2, lrows, :] = jnp.exp(dec - m_new)
        r_pad = r_all if pad_rows == 0 else jnp.concatenate([r_all, jnp.zeros((pad_rows, LANES), F32)], axis=0)
        rt_ref[c] = r_pad.T
        w_c_rows.append(jnp.exp(b_end + m_run - m_new))
        m_run = jnp.where(gate_lanes, m_new, 0.0)
    m_ref[...] = m_run

    def pieces():
        for c in range(n_ch):
            for h in range(M_HEADS):
                yield (c, h, c * M_HEADS + h, slice(c * length, (c + 1) * length), slice(h * m_dim, (h + 1) * m_dim),
                       GATE_LANE0 + h)

    for c, h, i, rows, hcols, gl in pieces():
        sqk_ref[i] = _dot_nt(qs_ref[rows, hcols], ks_ref[rows, hcols])

    for c, h, i, rows, hcols, gl in pieces():
        a_h = gq_ref[c, 1, lrows, gl:gl + 1]
        dmat = jnp.where(causal, gq_ref[c, 0, lrows, gl:gl + 1] + rt_ref[c, gl:gl + 1, lrows], NEG_INF)
        m_t = jnp.maximum(a_h, jnp.max(dmat, axis=-1, keepdims=True))
        m_rep = jnp.broadcast_to(m_t, (length, LANES))
        a_rep = jnp.broadcast_to(a_h, (length, LANES))
        m_wide = jnp.tile(m_rep, (1, length // LANES)) if length % LANES == 0 else m_t
        s = sqk_ref[i] * jnp.exp(dmat - m_wide)
        sbf_ref[i] = s.astype(BF16)
        rs_ref[i] = jnp.broadcast_to(jnp.sum(s, axis=-1, keepdims=True), (length, LANES))
        wi_ref[i] = jnp.exp(a_rep - m_rep)
        em_ref[i] = jnp.exp(-m_rep)
        w_s = gq_ref[c, 2, lrows, gl:gl + 1]
        vh = z_ref[rows, hcols]
        vb_ref[rows, hcols] = vh.astype(BF16)
        vw_ref[rows, hcols] = (vh * w_s).astype(BF16)
        dn_ref[i:i + 1, :] = jnp.sum(ks_ref[rows, hcols].astype(F32) * w_s, axis=0, keepdims=True)

    for c, h, i, rows, hcols, gl in pieces():
        num_ref[rows, hcols] = _dot(sbf_ref[i], vb_ref[rows, hcols])
        dc_ref[i] = _dot_tn(vw_ref[rows, hcols], ks_ref[rows, hcols])

    for c, h, i, rows, hcols, gl in pieces():
        w_c = w_c_rows[c][:, gl:gl + 1]
        c_h = c_ref[h]
        cst_ref[i] = c_h.astype(BF16)
        c_ref[h] = w_c * c_h + dc_ref[i]
        n_h = n_ref[h:h + 1, :]
        nst_ref[i:i + 1, :] = n_h
        n_ref[h:h + 1, :] = w_c * n_h + dn_ref[i:i + 1, :]

    for c, h, i, rows, hcols, gl in pieces():
        numc_ref[rows, hcols] = _dot_nt(qs_ref[rows, hcols], cst_ref[i])

    half_cols = m_dim // LANES
    for c, h, i, rows, hcols, gl in pieces():
        w_inter = wi_ref[i]
        qn = jnp.sum(qs_ref[rows, hcols].astype(F32) * nst_ref[i:i + 1, :], axis=-1, keepdims=True)
        den = rs_ref[i] + w_inter * qn
        inv = 1.0 / jnp.maximum(jnp.abs(den), em_ref[i])
        hh = []
        for j in range(half_cols):
            cols = slice(h * m_dim + j * LANES, h * m_dim + (j + 1) * LANES)
            hh.append((num_ref[rows, cols] + w_inter * numc_ref[rows, cols]) * inv)
        ms = sum(jnp.sum(v * v, axis=-1, keepdims=True) for v in hh) * (1.0 / m_dim)
        rstd = lax.rsqrt(ms + EPS)
        for j in range(half_cols):
            cols = slice(h * m_dim + j * LANES, h * m_dim + (j + 1) * LANES)
            og = _sigmoid(z_ref[rows, width + h * m_dim + j * LANES:width + h * m_dim + (j + 1) * LANES])
            mix_ref[rows, cols] = (hh[j] * rstd * ng_ref[:, cols] * og).astype(BF16)

    xo_ref[0] = x + _dot(mix_ref[...], wout_ref[...])

    for j in range(n_slab):
        qko_ref[0, :, j * LANES:(j + 1) * LANES] = qkext_ref[j, tile + QK_PAD - hist_rows:tile + QK_PAD, :]
    co_ref[0] = c_ref[...]
    no_ref[0] = n_ref[...]
    mo_ref[0] = m_ref[...]
    if n_tiles > 1:
        for j in range(n_slab):
            qkext_ref[j, 0:QK_PAD, :] = qkext_ref[j, tile:tile + QK_PAD, :]


def _odd_layer(x, g, w_in, w_gate, gate_bias, cw, cb, ng, w_out, hist, *, tile, m_chunk):
    bsz, t_len, d = x.shape
    tile = min(tile, t_len)
    m_chunk = min(m_chunk, tile)
    assert t_len % tile == 0 and tile % m_chunk == 0
    n_tiles = t_len // tile
    has_hist = hist is not None
    assert (not has_hist) or n_tiles == 1
    width = ng.shape[0]
    m_dim = width // M_HEADS
    hist_rows = QK_CONV_W - 1
    n_ch = tile // m_chunk
    n_hc = n_ch * M_HEADS

    in_specs = [
        pl.BlockSpec((1, tile, d), lambda b, t: (b, t, 0)),
        _const_spec((1, d)), _const_spec((d, 4 * width)), _const_spec((d, LANES)), _const_spec((1, LANES)),
        _const_spec((QK_CONV_W, 2 * width)), _const_spec((1, 2 * width)), _const_spec((1, width)),
        _const_spec((width, d)),
    ]
    args = [x, g.reshape(1, d), w_in, w_gate, gate_bias, cw, cb.reshape(1, 2 * width), ng.reshape(1, width), w_out]
    if has_hist:
        layer, qk_hist, c0, n0, m0 = hist
        m0_lanes = jnp.pad(m0[layer], ((0, 0), (GATE_LANE0, LANES - GATE_LANE0 - M_HEADS))).reshape(bsz, 1, LANES)
        in_specs += [pl.BlockSpec((None, 1, hist_rows, 2 * width), lambda b, t: (layer, b, 0, 0)),
                     pl.BlockSpec((None, 1, M_HEADS, m_dim, m_dim), lambda b, t: (layer, b, 0, 0, 0)),
                     pl.BlockSpec((None, 1, M_HEADS, m_dim), lambda b, t: (layer, b, 0, 0)),
                     pl.BlockSpec((1, 1, LANES), lambda b, t: (b, 0, 0))]
        args += [qk_hist, c0, n0, m0_lanes]
    out_specs = [pl.BlockSpec((1, tile, d), lambda b, t: (b, t, 0)),
                 pl.BlockSpec((1, hist_rows, 2 * width), lambda b, t: (b, 0, 0)),
                 pl.BlockSpec((1, M_HEADS, m_dim, m_dim), lambda b, t: (b, 0, 0, 0)),
                 pl.BlockSpec((1, M_HEADS, m_dim), lambda b, t: (b, 0, 0)),
                 pl.BlockSpec((1, 1, LANES), lambda b, t: (b, 0, 0))]
    out_shape = [jax.ShapeDtypeStruct((bsz, t_len, d), F32),
                 jax.ShapeDtypeStruct((bsz, hist_rows, 2 * width), F32),
                 jax.ShapeDtypeStruct((bsz, M_HEADS, m_dim, m_dim), F32),
                 jax.ShapeDtypeStruct((bsz, M_HEADS, m_dim), F32),
                 jax.ShapeDtypeStruct((bsz, 1, LANES), F32)]
    scratch = [pltpu.VMEM((2 * width // LANES, QK_PAD + tile, LANES), F32),
               pltpu.VMEM((tile, 2 * width), F32),
               pltpu.VMEM((tile, width), BF16),
               pltpu.VMEM((tile, width), BF16),
               pltpu.VMEM((tile, width), BF16),
               pltpu.VMEM((M_HEADS, m_dim, m_dim), F32),
               pltpu.VMEM((M_HEADS, m_dim), F32),
               pltpu.VMEM((1, LANES), F32),
               pltpu.VMEM((n_ch, 3, max(m_chunk, LANES), LANES), F32),
               pltpu.VMEM((n_ch, LANES, max(m_chunk, LANES)), F32),
               pltpu.VMEM((n_hc, m_chunk, m_chunk), F32),
               pltpu.VMEM((n_hc, m_chunk, m_chunk), BF16),
               pltpu.VMEM((n_hc, m_chunk, LANES), F32),
               pltpu.VMEM((n_hc, m_chunk, LANES), F32),
               pltpu.VMEM((n_hc, m_chunk, LANES), F32),
               pltpu.VMEM((tile, width), BF16),
               pltpu.VMEM((tile, width), BF16),
               pltpu.VMEM((n_hc, m_dim), F32),
               pltpu.VMEM((tile, width), F32),
               pltpu.VMEM((n_hc, m_dim, m_dim), F32),
               pltpu.VMEM((n_hc, m_dim, m_dim), BF16),
               pltpu.VMEM((n_hc, m_dim), F32),
               pltpu.VMEM((tile, width), F32)]
    body = functools.partial(_odd_body, tile=tile, m_chunk=m_chunk, has_hist=has_hist, n_tiles=n_tiles)
    xo, qko, co, no, mo = pl.pallas_call(
        body,
        grid=(bsz, n_tiles),
        in_specs=in_specs,
        out_specs=out_specs,
        out_shape=out_shape,
        scratch_shapes=scratch,
        compiler_params=pltpu.CompilerParams(dimension_semantics=("arbitrary", "arbitrary"),
                                             vmem_limit_bytes=VMEM_LIMIT_BYTES),
        name="mlstm_mixer_hist" if has_hist else "mlstm_mixer",
    )(*args)
    return xo, qko, co, no, mo[:, 0, GATE_LANE0:GATE_LANE0 + M_HEADS]


PROMPT_TILE = 512
MLSTM_CHUNK = 128
FFN_ROWS = 1024


def kernel(x_prompt, x_sample, cache_swa_k, cache_swa_v, state_conv, state_qk_conv, state_mlstm_C, state_mlstm_n,
           state_mlstm_m, norm_mix_g, norm_ffn_g, norm_out_g, w_in_even, attn_sink, conv_w, conv_b, conv_ln_g,
           conv_ln_b, w_out_even, w_in_odd, qk_conv_w, qk_conv_b, gate_b_i, gate_b_f, mlstm_norm_g, w_out_odd,
           w_ff1, w_ff2):
    depth = norm_mix_g.shape[0]
    bp, s_len, d = x_prompt.shape
    bs, t_len, _ = x_sample.shape
    width = mlstm_norm_g.shape[1]
    xp, xs = x_prompt, x_sample
    outs_p = [[] for _ in range(7)]
    outs_s = [[] for _ in range(7)]
    for l in range(depth):
        if l % 2 == 0:
            i = l // 2
            prm = (norm_mix_g[l], w_in_even[i].astype(BF16), attn_sink[i], conv_w[i], conv_b[i], conv_ln_g[i],
                   conv_ln_b[i], w_out_even[i].astype(BF16))
            xp, kp, vp, cvp = _even_layer(xp, *prm, None, 0, tile=PROMPT_TILE, q_chunk=CHUNK)
            xs, ks_, vs_, cvs = _even_layer(xs, *prm, (i, cache_swa_k, cache_swa_v, state_conv), PAST_LEN,
                                            tile=t_len, q_chunk=t_len)
            for lst, val in zip(outs_p[:3], (kp, vp, cvp)):
                lst.append(val)
            for lst, val in zip(outs_s[:3], (ks_, vs_, cvs)):
                lst.append(val)
        else:
            j = l // 2
            w_in = w_in_odd[j]
            n_gate = 2 * M_HEADS
            w_gate = jnp.pad(w_in[:, 4 * width:], ((0, 0), (0, LANES - n_gate))).astype(BF16)
            gate_bias = jnp.pad(jnp.concatenate([gate_b_i[j], gate_b_f[j]]), (0, LANES - n_gate)).reshape(1, LANES)
            prm = (norm_mix_g[l], w_in[:, :4 * width].astype(BF16), w_gate, gate_bias, qk_conv_w[j], qk_conv_b[j],
                   mlstm_norm_g[j], w_out_odd[j].astype(BF16))
            xp, qkp, cpn, npn, mpn = _odd_layer(xp, *prm, None, tile=PROMPT_TILE, m_chunk=MLSTM_CHUNK)
            xs, qks, csn, nsn, msn = _odd_layer(
                xs, *prm, (j, state_qk_conv, state_mlstm_C, state_mlstm_n, state_mlstm_m),
                tile=t_len, m_chunk=t_len)
            for lst, val in zip(outs_p[3:], (qkp, cpn, npn, mpn)):
                lst.append(val)
            for lst, val in zip(outs_s[3:], (qks, csn, nsn, msn)):
                lst.append(val)
        g_out = norm_out_g if l == depth - 1 else None
        w1 = w_ff1[l].astype(BF16)
        w2 = w_ff2[l].astype(BF16)
        xp = _ffn(xp.reshape(bp * s_len, d), norm_ffn_g[l], w1, w2, g_out, tile_rows=FFN_ROWS).reshape(bp, s_len, d)
        xs = _ffn(xs.reshape(bs * t_len, d), norm_ffn_g[l], w1, w2, g_out, tile_rows=FFN_ROWS).reshape(bs, t_len, d)
    return (xp, xs) + tuple(jnp.stack(v) for v in outs_p) + tuple(jnp.stack(v) for v in outs_s)
```

```python
import functools

import jax
import jax.numpy as jnp
from jax import lax
from jax.experimental import pallas as pl
from jax.experimental.pallas import tpu as pltpu

F32 = jnp.float32
BF16 = jnp.bfloat16

CHUNK = 64
A_HEADS = 8
A_KV_HEADS = 2
HEAD_DIM = 64
WINDOW = 128
ROPE_DIM = 16
ROPE_THETA = 500000.0
CONV_W = 31
M_HEADS = 4
QK_CONV_W = 4
PAST_LEN = 4096
EPS = 1e-6

LANES = 128
SUBLANES = 8
VMEM_LIMIT_BYTES = 56 * 1024 * 1024

NEG_INF = float("-inf")


def _rms(x, g):
    return x * lax.rsqrt(jnp.mean(x * x, axis=-1, keepdims=True) + EPS) * g


def _sigmoid(x):
    return 0.5 * jnp.tanh(0.5 * x) + 0.5


def _log_sigmoid(x):
    return jnp.minimum(x, 0.0) - jnp.log(1.0 + jnp.exp(-jnp.abs(x)))


def _dot(a, b):
    return jnp.dot(a, b, preferred_element_type=F32)


def _dot_nt(a, b):
    return lax.dot_general(a, b, (((1,), (1,)), ((), ())), preferred_element_type=F32)


def _dot_tn(a, b):
    return lax.dot_general(a, b, (((0,), (0,)), ((), ())), preferred_element_type=F32)


def _const_spec(shape):
    nd = len(shape)
    return pl.BlockSpec(shape, lambda *_: (0,) * nd, pipeline_mode=pl.Buffered(1))


def _ffn_body(*refs, n_hidden_chunks, hidden_chunk, final_norm):
    if final_norm:
        x_ref, g_ref, w1_ref, w2_ref, gout_ref, o_ref = refs
    else:
        x_ref, g_ref, w1_ref, w2_ref, o_ref = refs
    x = x_ref[...]
    xb = _rms(x, g_ref[...]).astype(BF16)
    acc = x
    for c in range(n_hidden_chunks):
        cols = slice(c * hidden_chunk, (c + 1) * hidden_chunk)
        h = jnp.maximum(_dot(xb, w1_ref[:, cols]), 0.0)
        acc = acc + _dot((h * h).astype(BF16), w2_ref[cols, :])
    if final_norm:
        acc = _rms(acc, gout_ref[...])
    o_ref[...] = acc


def _ffn(x, g, w1, w2, g_out, *, tile_rows):
    n, d = x.shape
    d_ff = w1.shape[1]
    hidden_chunk = min(d_ff, 1024)
    final_norm = g_out is not None
    tile_rows = min(tile_rows, n)
    assert n % tile_rows == 0 and d_ff % hidden_chunk == 0
    body = functools.partial(_ffn_body, n_hidden_chunks=d_ff // hidden_chunk, hidden_chunk=hidden_chunk,
                             final_norm=final_norm)
    in_specs = [pl.BlockSpec((tile_rows, d), lambda i: (i, 0)), _const_spec((1, d)), _const_spec((d, d_ff)),
                _const_spec((d_ff, d))]
    args = [x, g.reshape(1, d), w1, w2]
    if final_norm:
        in_specs.append(_const_spec((1, d)))
        args.append(g_out.reshape(1, d))
    return pl.pallas_call(
        body,
        grid=(n // tile_rows,),
        in_specs=in_specs,
        out_specs=pl.BlockSpec((tile_rows, d), lambda i: (i, 0)),
        out_shape=jax.ShapeDtypeStruct((n, d), F32),
        compiler_params=pltpu.CompilerParams(dimension_semantics=("arbitrary",), vmem_limit_bytes=VMEM_LIMIT_BYTES),
        name="ffn_final" if final_norm else "ffn",
    )(*args)


CONV_PAD = 32
CONV_ROWS = 64


def _even_body(*refs, tile, q_chunk, has_hist, n_tiles):
    if has_hist:
        (x_ref, g_ref, win_ref, cos_ref, sa_ref, sb_ref, sink_ref, cw_ref, cb_ref, lng_ref, lnb_ref, wout_ref,
         kh_ref, vh_ref, ch_ref, xo_ref, ko_ref, vo_ref, co_ref,
         z_ref, q_ref, kext_ref, vext_ref, kvar_ref, vvar_ref, uext_ref, mix_ref, s_ref, p_ref) = refs
    else:
        (x_ref, g_ref, win_ref, cos_ref, sa_ref, sb_ref, sink_ref, cw_ref, cb_ref, lng_ref, lnb_ref, wout_ref,
         xo_ref, ko_ref, vo_ref, co_ref,
         z_ref, q_ref, kext_ref, vext_ref, kvar_ref, vvar_ref, uext_ref, mix_ref, s_ref, p_ref) = refs
    t = pl.program_id(1)
    a_q = A_HEADS * HEAD_DIM
    a_kv = A_KV_HEADS * HEAD_DIM
    conv_ch = cw_ref.shape[1]
    n_slab = conv_ch // LANES
    n_keys = WINDOW + q_chunk
    hist_rows = CONV_W - 1

    @pl.when(t == 0)
    def _init():
        for j in range(n_slab):
            uext_ref[j, 0:CONV_PAD, :] = jnp.zeros((CONV_PAD, LANES), F32)
        if has_hist:
            kext_ref[0:WINDOW, :] = kh_ref[0]
            vext_ref[0:WINDOW, :] = vh_ref[0]
            for j in range(n_slab):
                uext_ref[j, CONV_PAD - hist_rows:CONV_PAD, :] = ch_ref[0, :, j * LANES:(j + 1) * LANES]
        else:
            kext_ref[0:WINDOW, :] = jnp.zeros((WINDOW, a_kv), F32)
            vext_ref[0:WINDOW, :] = jnp.zeros((WINDOW, a_kv), F32)

    x = x_ref[0]
    hn = _rms(x, g_ref[...]).astype(BF16)
    z_ref[...] = _dot(hn, win_ref[...])

    cos_t = cos_ref[...]
    sin_a = sa_ref[...]
    sin_b = sb_ref[...]
    half = ROPE_DIM // 2

    def rope(v):
        return v * cos_t + pltpu.roll(v, LANES - half, 1) * sin_a + pltpu.roll(v, half, 1) * sin_b

    scale = HEAD_DIM ** -0.5
    for j in range(a_q // LANES):
        cols = slice(j * LANES, (j + 1) * LANES)
        q_ref[:, cols] = (rope(z_ref[:, cols]) * scale).astype(BF16)
    kext_ref[WINDOW:WINDOW + tile, :] = rope(z_ref[:, a_q:a_q + a_kv])
    vext_ref[WINDOW:WINDOW + tile, :] = z_ref[:, a_q + a_kv:a_q + 2 * a_kv]

    lane = lax.broadcasted_iota(jnp.int32, (1, LANES), 1)
    low = lane < HEAD_DIM
    for src_ref, var_ref in ((kext_ref, kvar_ref), (vext_ref, vvar_ref)):
        kx = src_ref[...]
        kr = pltpu.roll(kx, HEAD_DIM, 1)
        var_ref[0] = jnp.where(low, kx, 0.0).astype(BF16)
        var_ref[1] = jnp.where(low, 0.0, kr).astype(BF16)
        var_ref[2] = jnp.where(low, kr, 0.0).astype(BF16)
        var_ref[3] = jnp.where(low, 0.0, kx).astype(BF16)

    group = A_HEADS // A_KV_HEADS
    row = lax.broadcasted_iota(jnp.int32, (2 * q_chunk, 1), 0)
    first_rows = row < q_chunk

    n_chunks = tile // q_chunk
    for c in range(n_chunks):
        r0 = c * q_chunk
        for g in range(A_KV_HEADS):
            qa = q_ref[r0:r0 + q_chunk, 2 * g * LANES:(2 * g + 1) * LANES]
            qb = q_ref[r0:r0 + q_chunk, (2 * g + 1) * LANES:(2 * g + 2) * LANES]
            qs = jnp.concatenate([qa, qb], axis=0)
            for hi in range(2):
                s_ref[(c * A_KV_HEADS + g) * 2 + hi] = _dot_nt(qs, kvar_ref[2 * g + hi, r0:r0 + n_keys, :])
    for c in range(n_chunks):
        r0 = c * q_chunk
        masked = (not has_hist) and r0 < WINDOW
        if masked:
            key_pos = t * tile + r0 - WINDOW + lax.broadcasted_iota(jnp.int32, (1, n_keys), 1)
            bias = jnp.where(key_pos >= 0, 0.0, NEG_INF)
        for g in range(A_KV_HEADS):
            for hi in range(2):
                i = (c * A_KV_HEADS + g) * 2 + hi
                s = s_ref[i]
                if masked:
                    s = s + bias
                sk = jnp.where(first_rows, sink_ref[group * g + hi], sink_ref[group * g + 2 + hi])
                mx = jnp.maximum(jnp.max(s, axis=-1, keepdims=True), sk)
                p = jnp.exp(s - mx)
                den = jnp.sum(p, axis=-1, keepdims=True) + jnp.exp(sk - mx)
                p_ref[i] = (p * (1.0 / den)).astype(BF16)
    for c in range(n_chunks):
        r0 = c * q_chunk
        for g in range(A_KV_HEADS):
            i = (c * A_KV_HEADS + g) * 2
            acc = (_dot(p_ref[i], vvar_ref[2 * g, r0:r0 + n_keys, :])
                   + _dot(p_ref[i + 1], vvar_ref[2 * g + 1, r0:r0 + n_keys, :]))
            mix_ref[r0:r0 + q_chunk, 2 * g * LANES:(2 * g + 1) * LANES] = acc[0:q_chunk].astype(BF16)
            mix_ref[r0:r0 + q_chunk, (2 * g + 1) * LANES:(2 * g + 2) * LANES] = acc[q_chunk:].astype(BF16)

    glu0 = a_q + 2 * a_kv
    for j in range(n_slab):
        lo = glu0 + j * LANES
        uext_ref[j, CONV_PAD:CONV_PAD + tile, :] = (
            z_ref[:, lo:lo + LANES] * _sigmoid(z_ref[:, lo + conv_ch:lo + conv_ch + LANES]))
    conv_rows = min(CONV_ROWS, tile)
    for i in range(tile // conv_rows):
        r0 = i * conv_rows
        accs = []
        for j in range(n_slab):
            cols = slice(j * LANES, (j + 1) * LANES)
            acc = jnp.zeros((conv_rows, LANES), F32)
            for w in range(CONV_W):
                acc = acc + uext_ref[j, pl.ds(r0 + CONV_PAD - hist_rows + w, conv_rows), :] * cw_ref[w:w + 1, cols]
            accs.append(acc + cb_ref[:, cols])
        mean = sum(jnp.sum(a, axis=-1, keepdims=True) for a in accs) * (1.0 / conv_ch)
        cent = [a - mean for a in accs]
        var = sum(jnp.sum(xc * xc, axis=-1, keepdims=True) for xc in cent) * (1.0 / conv_ch)
        rstd = lax.rsqrt(var + EPS)
        for j in range(n_slab):
            cols = slice(j * LANES, (j + 1) * LANES)
            y = cent[j] * rstd * lng_ref[:, cols] + lnb_ref[:, cols]
            mix_ref[r0:r0 + conv_rows, a_q + j * LANES:a_q + (j + 1) * LANES] = (y * _sigmoid(y)).astype(BF16)

    xo_ref[0] = x + _dot(mix_ref[...], wout_ref[...])

    ko_ref[0] = kext_ref[tile:tile + WINDOW, :]
    vo_ref[0] = vext_ref[tile:tile + WINDOW, :]
    for j in range(n_slab):
        co_ref[0, :, j * LANES:(j + 1) * LANES] = uext_ref[j, tile + CONV_PAD - hist_rows:tile + CONV_PAD, :]
    if n_tiles > 1:
        kext_ref[0:WINDOW, :] = kext_ref[tile:tile + WINDOW, :]
        vext_ref[0:WINDOW, :] = vext_ref[tile:tile + WINDOW, :]
        for j in range(n_slab):
            uext_ref[j, 0:CONV_PAD, :] = uext_ref[j, tile:tile + CONV_PAD, :]


def _rope_tables(pos0, t_len):
    half = ROPE_DIM // 2
    pos = pos0 + jnp.arange(t_len, dtype=jnp.int32)
    inv = ROPE_THETA ** (-jnp.arange(half, dtype=F32) / half)
    ang = pos.astype(F32)[:, None] * inv[None, :]
    cos = jnp.cos(ang)
    sin = jnp.sin(ang)
    ones = jnp.ones((t_len, HEAD_DIM - ROPE_DIM), F32)
    zeros = jnp.zeros((t_len, HEAD_DIM - ROPE_DIM), F32)
    zh = jnp.zeros((t_len, half), F32)
    reps = LANES // HEAD_DIM
    cos_t = jnp.tile(jnp.concatenate([cos, cos, ones], axis=1), (1, reps))
    sin_a = jnp.tile(jnp.concatenate([-sin, zh, zeros], axis=1), (1, reps))
    sin_b = jnp.tile(jnp.concatenate([zh, sin, zeros], axis=1), (1, reps))
    return cos_t, sin_a, sin_b


def _even_layer(x, g, w_in, sink, cw, cb, lng, lnb, w_out, hist, pos0, *, tile, q_chunk):
    bsz, t_len, d = x.shape
    tile = min(tile, t_len)
    assert t_len % tile == 0 and tile % q_chunk == 0
    n_tiles = t_len // tile
    has_hist = hist is not None
    assert has_hist or q_chunk == CHUNK
    assert (not has_hist) or n_tiles == 1
    e_in = w_in.shape[1]
    conv_ch = cw.shape[1]
    a_kv = A_KV_HEADS * HEAD_DIM
    a_q = A_HEADS * HEAD_DIM
    hist_rows = CONV_W - 1
    cos_t, sin_a, sin_b = _rope_tables(pos0, t_len)

    tab_spec = pl.BlockSpec((tile, LANES), lambda b, t: (t, 0))
    in_specs = [
        pl.BlockSpec((1, tile, d), lambda b, t: (b, t, 0)),
        _const_spec((1, d)), _const_spec((d, e_in)),
        tab_spec, tab_spec, tab_spec,
        pl.BlockSpec(memory_space=pltpu.SMEM),
        _const_spec((CONV_W, conv_ch)), _const_spec((1, conv_ch)), _const_spec((1, conv_ch)),
        _const_spec((1, conv_ch)), _const_spec((a_q + conv_ch, d)),
    ]
    args = [x, g.reshape(1, d), w_in, cos_t, sin_a, sin_b, sink, cw, cb.reshape(1, conv_ch),
            lng.reshape(1, conv_ch), lnb.reshape(1, conv_ch), w_out]
    if has_hist:
        layer, k_hist, v_hist, c_hist = hist
        in_specs += [pl.BlockSpec((1, WINDOW, a_kv), lambda b, t: (b, 0, 0)),
                     pl.BlockSpec((1, WINDOW, a_kv), lambda b, t: (b, 0, 0)),
                     pl.BlockSpec((None, 1, hist_rows, conv_ch), lambda b, t: (layer, b, 0, 0))]
        args += [k_hist[layer].reshape(bsz, WINDOW, a_kv), v_hist[layer].reshape(bsz, WINDOW, a_kv), c_hist]
    out_specs = [pl.BlockSpec((1, tile, d), lambda b, t: (b, t, 0)),
                 pl.BlockSpec((1, WINDOW, a_kv), lambda b, t: (b, 0, 0)),
                 pl.BlockSpec((1, WINDOW, a_kv), lambda b, t: (b, 0, 0)),
                 pl.BlockSpec((1, hist_rows, conv_ch), lambda b, t: (b, 0, 0))]
    out_shape = [jax.ShapeDtypeStruct((bsz, t_len, d), F32),
                 jax.ShapeDtypeStruct((bsz, WINDOW, a_kv), F32),
                 jax.ShapeDtypeStruct((bsz, WINDOW, a_kv), F32),
                 jax.ShapeDtypeStruct((bsz, hist_rows, conv_ch), F32)]
    n_blocks = 2 * A_KV_HEADS * (tile // q_chunk)
    scratch = [pltpu.VMEM((tile, e_in), F32),
               pltpu.VMEM((tile, a_q), BF16),
               pltpu.VMEM((WINDOW + tile, a_kv), F32),
               pltpu.VMEM((WINDOW + tile, a_kv), F32),
               pltpu.VMEM((4, WINDOW + tile, LANES), BF16),
               pltpu.VMEM((4, WINDOW + tile, LANES), BF16),
               pltpu.VMEM((conv_ch // LANES, CONV_PAD + tile, LANES), F32),
               pltpu.VMEM((tile, a_q + conv_ch), BF16),
               pltpu.VMEM((n_blocks, 2 * q_chunk, WINDOW + q_chunk), F32),
               pltpu.VMEM((n_blocks, 2 * q_chunk, WINDOW + q_chunk), BF16)]
    body = functools.partial(_even_body, tile=tile, q_chunk=q_chunk, has_hist=has_hist, n_tiles=n_tiles)
    xo, ko, vo, co = pl.pallas_call(
        body,
        grid=(bsz, n_tiles),
        in_specs=in_specs,
        out_specs=out_specs,
        out_shape=out_shape,
        scratch_shapes=scratch,
        compiler_params=pltpu.CompilerParams(dimension_semantics=("arbitrary", "arbitrary"),
                                             vmem_limit_bytes=VMEM_LIMIT_BYTES),
        name="even_mixer_hist" if has_hist else "even_mixer",
    )(*args)
    kv_shape = (bsz, WINDOW, A_KV_HEADS, HEAD_DIM)
    return xo, ko.reshape(kv_shape), vo.reshape(kv_shape), co


QK_PAD = 8
QK_CONV_ROWS = 128
GATE_LANE0 = M_HEADS


def _cumsum_rows(x):
    n = x.shape[0]
    tri = (lax.broadcasted_iota(jnp.int32, (n, n), 0) >= lax.broadcasted_iota(jnp.int32, (n, n), 1)).astype(BF16)
    x1 = x.astype(BF16)
    r1 = x - x1.astype(F32)
    x2 = r1.astype(BF16)
    x3 = (r1 - x2.astype(F32)).astype(BF16)
    return _dot(tri, x1) + _dot(tri, x2) + _dot(tri, x3)


def _odd_body(*refs, tile, m_chunk, has_hist, n_tiles):
    if has_hist:
        (x_ref, g_ref, win_ref, wg_ref, gb_ref, cw_ref, cb_ref, ng_ref, wout_ref,
         qkh_ref, c0_ref, n0_ref, m0_ref, xo_ref, qko_ref, co_ref, no_ref, mo_ref,
         qkext_ref, z_ref, qs_ref, ks_ref, mix_ref, c_ref, n_ref, m_ref,
         gq_ref, rt_ref, sqk_ref, sbf_ref, rs_ref, wi_ref, em_ref, vb_ref, vw_ref, dn_ref, num_ref, dc_ref, cst_ref, nst_ref,
         numc_ref) = refs
    else:
        (x_ref, g_ref, win_ref, wg_ref, gb_ref, cw_ref, cb_ref, ng_ref, wout_ref,
         xo_ref, qko_ref, co_ref, no_ref, mo_ref,
         qkext_ref, z_ref, qs_ref, ks_ref, mix_ref, c_ref, n_ref, m_ref,
         gq_ref, rt_ref, sqk_ref, sbf_ref, rs_ref, wi_ref, em_ref, vb_ref, vw_ref, dn_ref, num_ref, dc_ref, cst_ref, nst_ref,
         numc_ref) = refs
    t = pl.program_id(1)
    width = ng_ref.shape[1]
    m_dim = width // M_HEADS
    n_slab = 2 * width // LANES
    hist_rows = QK_CONV_W - 1
    length = m_chunk

    @pl.when(t == 0)
    def _init():
        for j in range(n_slab):
            qkext_ref[j, 0:QK_PAD, :] = jnp.zeros((QK_PAD, LANES), F32)
        if has_hist:
            for j in range(n_slab):
                qkext_ref[j, QK_PAD - hist_rows:QK_PAD, :] = qkh_ref[0, :, j * LANES:(j + 1) * LANES]
            c_ref[...] = c0_ref[0]
            n_ref[...] = n0_ref[0]
            m_ref[...] = m0_ref[0]
        else:
            c_ref[...] = jnp.zeros(c_ref.shape, F32)
            n_ref[...] = jnp.zeros(n_ref.shape, F32)
            m_ref[...] = jnp.zeros(m_ref.shape, F32)

    x = x_ref[0]
    hn = _rms(x, g_ref[...]).astype(BF16)
    for n in range(n_slab // 2):
        r = _dot(hn, win_ref[:, 2 * n * LANES:(2 * n + 2) * LANES])
        qkext_ref[2 * n, QK_PAD:QK_PAD + tile, :] = r[:, 0:LANES]
        qkext_ref[2 * n + 1, QK_PAD:QK_PAD + tile, :] = r[:, LANES:2 * LANES]
    z_ref[...] = _dot(hn, win_ref[:, 2 * width:4 * width])
    gts = _dot(hn, wg_ref[...]) + gb_ref[...]

    conv_rows = min(QK_CONV_ROWS, tile)
    k_scale = m_dim ** -0.5
    for j in range(n_slab):
        cols = slice(j * LANES, (j + 1) * LANES)
        is_q = j * LANES < width
        dst_ref = qs_ref if is_q else ks_ref
        dcols = slice(j * LANES % width, j * LANES % width + LANES)
        for i in range(tile // conv_rows):
            r0 = i * conv_rows
            acc = jnp.zeros((conv_rows, LANES), F32)
            for w in range(QK_CONV_W):
                acc = acc + qkext_ref[j, pl.ds(r0 + QK_PAD - hist_rows + w, conv_rows), :] * cw_ref[w:w + 1, cols]
            acc = acc + cb_ref[:, cols]
            y = acc * _sigmoid(acc)
            if not is_q:
                y = y * k_scale
            dst_ref[r0:r0 + conv_rows, dcols] = y.astype(BF16)

    lane = lax.broadcasted_iota(jnp.int32, (1, LANES), 1)
    gate_lanes = (lane >= GATE_LANE0) & (lane < GATE_LANE0 + M_HEADS)
    lf_all = jnp.where(gate_lanes, _log_sigmoid(gts), 0.0)
    ig_all = jnp.where(gate_lanes, pltpu.roll(gts, GATE_LANE0, 1), 0.0)
    causal = (lax.broadcasted_iota(jnp.int32, (length, length), 0)
              >= lax.broadcasted_iota(jnp.int32, (length, length), 1))
    pad_rows = LANES - length if length < LANES else 0
    n_ch = tile // length
    lrows = slice(0, length)


    m_run = m_ref[...]
    w_c_rows = []
    for c in range(n_ch):
        rows = slice(c * length, (c + 1) * length)
        bcum = _cumsum_rows(lf_all[rows])
        b_end = bcum[length - 1:length, :]
        r_all = ig_all[rows] - bcum
        dec = b_end + r_all
        m_new = jnp.maximum(b_end + m_run, jnp.max(dec, axis=0, keepdims=True))
        gq_ref[c, 0, lrows, :] = bcum
        gq_ref[c, 1, lrows, :] = bcum + m_run
        gq_ref[c, 2, lrows, :] = jnp.exp(dec - m_new)
        r_pad = r_all if pad_rows == 0 else jnp.concatenate([r_all, jnp.zeros((pad_rows, LANES), F32)], axis=0)
        rt_ref[c] = r_pad.T
        w_c_rows.append(jnp.exp(b_end + m_run - m_new))
        m_run = jnp.where(gate_lanes, m_new, 0.0)
    m_ref[...] = m_run

    def pieces():
        for c in range(n_ch):
            for h in range(M_HEADS):
                yield (c, h, c * M_HEADS + h, slice(c * length, (c + 1) * length), slice(h * m_dim, (h + 1) * m_dim),
                       GATE_LANE0 + h)

    for c, h, i, rows, hcols, gl in pieces():
        sqk_ref[i] = _dot_nt(qs_ref[rows, hcols], ks_ref[rows, hcols])

    for c, h, i, rows, hcols, gl in pieces():
        a_h = gq_ref[c, 1, lrows, gl:gl + 1]
        dmat = jnp.where(causal, gq_ref[c, 0, lrows, gl:gl + 1] + rt_ref[c, gl:gl + 1, lrows], NEG_INF)
        m_t = jnp.maximum(a_h, jnp.max(dmat, axis=-1, keepdims=True))
        m_rep = jnp.broadcast_to(m_t, (length, LANES))
        a_rep = jnp.broadcast_to(a_h, (length, LANES))
        m_wide = jnp.tile(m_rep, (1, length // LANES)) if length % LANES == 0 else m_t
        s = sqk_ref[i] * jnp.exp(dmat - m_wide)
        sbf_ref[i] = s.astype(BF16)
        rs_ref[i] = jnp.broadcast_to(jnp.sum(s, axis=-1, keepdims=True), (length, LANES))
        wi_ref[i] = jnp.exp(a_rep - m_rep)
        em_ref[i] = jnp.exp(-m_rep)
        w_s = gq_ref[c, 2, lrows, gl:gl + 1]
        vh = z_ref[rows, hcols]
        vb_ref[rows, hcols] = vh.astype(BF16)
        vw_ref[rows, hcols] = (vh * w_s).astype(BF16)
        dn_ref[i:i + 1, :] = jnp.sum(ks_ref[rows, hcols].astype(F32) * w_s, axis=0, keepdims=True)

    for c, h, i, rows, hcols, gl in pieces():
        num_ref[rows, hcols] = _dot(sbf_ref[i], vb_ref[rows, hcols])
        dc_ref[i] = _dot_tn(vw_ref[rows, hcols], ks_ref[rows, hcols])

    for c, h, i, rows, hcols, gl in pieces():
        w_c = w_c_rows[c][:, gl:gl + 1]
        c_h = c_ref[h]
        cst_ref[i] = c_h.astype(BF16)
        c_ref[h] = w_c * c_h + dc_ref[i]
        n_h = n_ref[h:h + 1, :]
        nst_ref[i:i + 1, :] = n_h
        n_ref[h:h + 1, :] = w_c * n_h + dn_ref[i:i + 1, :]

    for c, h, i, rows, hcols, gl in pieces():
        numc_ref[rows, hcols] = _dot_nt(qs_ref[rows, hcols], cst_ref[i])

    half_cols = m_dim // LANES
    for c, h, i, rows, hcols, gl in pieces():
        w_inter = wi_ref[i]
        qn = jnp.sum(qs_ref[rows, hcols].astype(F32) * nst_ref[i:i + 1, :], axis=-1, keepdims=True)
        den = rs_ref[i] + w_inter * qn
        inv = 1.0 / jnp.maximum(jnp.abs(den), em_ref[i])
        hh = []
        for j in range(half_cols):
            cols = slice(h * m_dim + j * LANES, h * m_dim + (j + 1) * LANES)
            hh.append((num_ref[rows, cols] + w_inter * numc_ref[rows, cols]) * inv)
        ms = sum(jnp.sum(v * v, axis=-1, keepdims=True) for v in hh) * (1.0 / m_dim)
        rstd = lax.rsqrt(ms + EPS)
        for j in range(half_cols):
            cols = slice(h * m_dim + j * LANES, h * m_dim + (j + 1) * LANES)
            og = _sigmoid(z_ref[rows, width + h * m_dim + j * LANES:width + h * m_dim + (j + 1) * LANES])
            mix_ref[rows, cols] = (hh[j] * rstd * ng_ref[:, cols] * og).astype(BF16)

    xo_ref[0] = x + _dot(mix_ref[...], wout_ref[...])

    for j in range(n_slab):
        qko_ref[0, :, j * LANES:(j + 1) * LANES] = qkext_ref[j, tile + QK_PAD - hist_rows:tile + QK_PAD, :]
    co_ref[0] = c_ref[...]
    no_ref[0] = n_ref[...]
    mo_ref[0] = m_ref[...]
    if n_tiles > 1:
        for j in range(n_slab):
            qkext_ref[j, 0:QK_PAD, :] = qkext_ref[j, tile:tile + QK_PAD, :]


def _odd_layer(x, g, w_in, w_gate, gate_bias, cw, cb, ng, w_out, hist, *, tile, m_chunk):
    bsz, t_len, d = x.shape
    tile = min(tile, t_len)
    m_chunk = min(m_chunk, tile)
    assert t_len % tile == 0 and tile % m_chunk == 0
    n_tiles = t_len // tile
    has_hist = hist is not None
    assert (not has_hist) or n_tiles == 1
    width = ng.shape[0]
    m_dim = width // M_HEADS
    hist_rows = QK_CONV_W - 1
    n_ch = tile // m_chunk
    n_hc = n_ch * M_HEADS

    in_specs = [
        pl.BlockSpec((1, tile, d), lambda b, t: (b, t, 0)),
        _const_spec((1, d)), _const_spec((d, 4 * width)), _const_spec((d, LANES)), _const_spec((1, LANES)),
        _const_spec((QK_CONV_W, 2 * width)), _const_spec((1, 2 * width)), _const_spec((1, width)),
        _const_spec((width, d)),
    ]
    args = [x, g.reshape(1, d), w_in, w_gate, gate_bias, cw, cb.reshape(1, 2 * width), ng.reshape(1, width), w_out]
    if has_hist:
        layer, qk_hist, c0, n0, m0 = hist
        m0_lanes = jnp.pad(m0[layer], ((0, 0), (GATE_LANE0, LANES - GATE_LANE0 - M_HEADS))).reshape(bsz, 1, LANES)
        in_specs += [pl.BlockSpec((None, 1, hist_rows, 2 * width), lambda b, t: (layer, b, 0, 0)),
                     pl.BlockSpec((None, 1, M_HEADS, m_dim, m_dim), lambda b, t: (layer, b, 0, 0, 0)),
                     pl.BlockSpec((None, 1, M_HEADS, m_dim), lambda b, t: (layer, b, 0, 0)),
                     pl.BlockSpec((1, 1, LANES), lambda b, t: (b, 0, 0))]
        args += [qk_hist, c0, n0, m0_lanes]
    out_specs = [pl.BlockSpec((1, tile, d), lambda b, t: (b, t, 0)),
                 pl.BlockSpec((1, hist_rows, 2 * width), lambda b, t: (b, 0, 0)),
                 pl.BlockSpec((1, M_HEADS, m_dim, m_dim), lambda b, t: (b, 0, 0, 0)),
                 pl.BlockSpec((1, M_HEADS, m_dim), lambda b, t: (b, 0, 0)),
                 pl.BlockSpec((1, 1, LANES), lambda b, t: (b, 0, 0))]
    out_shape = [jax.ShapeDtypeStruct((bsz, t_len, d), F32),
                 jax.ShapeDtypeStruct((bsz, hist_rows, 2 * width), F32),
                 jax.ShapeDtypeStruct((bsz, M_HEADS, m_dim, m_dim), F32),
                 jax.ShapeDtypeStruct((bsz, M_HEADS, m_dim), F32),
                 jax.ShapeDtypeStruct((bsz, 1, LANES), F32)]
    scratch = [pltpu.VMEM((2 * width // LANES, QK_PAD + tile, LANES), F32),
               pltpu.VMEM((tile, 2 * width), F32),
               pltpu.VMEM((tile, width), BF16),
               pltpu.VMEM((tile, width), BF16),
               pltpu.VMEM((tile, width), BF16),
               pltpu.VMEM((M_HEADS, m_dim, m_dim), F32),
               pltpu.VMEM((M_HEADS, m_dim), F32),
               pltpu.VMEM((1, LANES), F32),
               pltpu.VMEM((n_ch, 3, max(m_chunk, LANES), LANES), F32),
               pltpu.VMEM((n_ch, LANES, max(m_chunk, LANES)), F32),
               pltpu.VMEM((n_hc, m_chunk, m_chunk), F32),
               pltpu.VMEM((n_hc, m_chunk, m_chunk), BF16),
               pltpu.VMEM((n_hc, m_chunk, LANES), F32),
               pltpu.VMEM((n_hc, m_chunk, LANES), F32),
               pltpu.VMEM((n_hc, m_chunk, LANES), F32),
               pltpu.VMEM((tile, width), BF16),
               pltpu.VMEM((tile, width), BF16),
               pltpu.VMEM((n_hc, m_dim), F32),
               pltpu.VMEM((tile, width), F32),
               pltpu.VMEM((n_hc, m_dim, m_dim), F32),
               pltpu.VMEM((n_hc, m_dim, m_dim), BF16),
               pltpu.VMEM((n_hc, m_dim), F32),
               pltpu.VMEM((tile, width), F32)]
    body = functools.partial(_odd_body, tile=tile, m_chunk=m_chunk, has_hist=has_hist, n_tiles=n_tiles)
    xo, qko, co, no, mo = pl.pallas_call(
        body,
        grid=(bsz, n_tiles),
        in_specs=in_specs,
        out_specs=out_specs,
        out_shape=out_shape,
        scratch_shapes=scratch,
        compiler_params=pltpu.CompilerParams(dimension_semantics=("arbitrary", "arbitrary"),
                                             vmem_limit_bytes=VMEM_LIMIT_BYTES),
        name="mlstm_mixer_hist" if has_hist else "mlstm_mixer",
    )(*args)
    return xo, qko, co, no, mo[:, 0, GATE_LANE0:GATE_LANE0 + M_HEADS]


EVEN_TILE = 1024
ODD_TILE = 512
MLSTM_CHUNK = 128
FFN_ROWS = 1024


def kernel(x_prompt, x_sample, cache_swa_k, cache_swa_v, state_conv, state_qk_conv, state_mlstm_C, state_mlstm_n,
           state_mlstm_m, norm_mix_g, norm_ffn_g, norm_out_g, w_in_even, attn_sink, conv_w, conv_b, conv_ln_g,
           conv_ln_b, w_out_even, w_in_odd, qk_conv_w, qk_conv_b, gate_b_i, gate_b_f, mlstm_norm_g, w_out_odd,
           w_ff1, w_ff2):
    depth = norm_mix_g.shape[0]
    bp, s_len, d = x_prompt.shape
    bs, t_len, _ = x_sample.shape
    width = mlstm_norm_g.shape[1]
    xp, xs = x_prompt, x_sample
    outs_p = [[] for _ in range(7)]
    outs_s = [[] for _ in range(7)]
    for l in range(depth):
        if l % 2 == 0:
            i = l // 2
            prm = (norm_mix_g[l], w_in_even[i].astype(BF16), attn_sink[i], conv_w[i], conv_b[i], conv_ln_g[i],
                   conv_ln_b[i], w_out_even[i].astype(BF16))
            xp, kp, vp, cvp = _even_layer(xp, *prm, None, 0, tile=EVEN_TILE, q_chunk=CHUNK)
            xs, ks_, vs_, cvs = _even_layer(xs, *prm, (i, cache_swa_k, cache_swa_v, state_conv), PAST_LEN,
                                            tile=t_len, q_chunk=t_len)
            for lst, val in zip(outs_p[:3], (kp, vp, cvp)):
                lst.append(val)
            for lst, val in zip(outs_s[:3], (ks_, vs_, cvs)):
                lst.append(val)
        else:
            j = l // 2
            w_in = w_in_odd[j]
            n_gate = 2 * M_HEADS
            w_gate = jnp.pad(w_in[:, 4 * width:], ((0, 0), (0, LANES - n_gate))).astype(BF16)
            gate_bias = jnp.pad(jnp.concatenate([gate_b_i[j], gate_b_f[j]]), (0, LANES - n_gate)).reshape(1, LANES)
            prm = (norm_mix_g[l], w_in[:, :4 * width].astype(BF16), w_gate, gate_bias, qk_conv_w[j], qk_conv_b[j],
                   mlstm_norm_g[j], w_out_odd[j].astype(BF16))
            xp, qkp, cpn, npn, mpn = _odd_layer(xp, *prm, None, tile=ODD_TILE, m_chunk=MLSTM_CHUNK)
            xs, qks, csn, nsn, msn = _odd_layer(
                xs, *prm, (j, state_qk_conv, state_mlstm_C, state_mlstm_n, state_mlstm_m),
                tile=t_len, m_chunk=t_len)
            for lst, val in zip(outs_p[3:], (qkp, cpn, npn, mpn)):
                lst.append(val)
            for lst, val in zip(outs_s[3:], (qks, csn, nsn, msn)):
                lst.append(val)
        g_out = norm_out_g if l == depth - 1 else None
        w1 = w_ff1[l].astype(BF16)
        w2 = w_ff2[l].astype(BF16)
        xp = _ffn(xp.reshape(bp * s_len, d), norm_ffn_g[l], w1, w2, g_out, tile_rows=FFN_ROWS).reshape(bp, s_len, d)
        xs = _ffn(xs.reshape(bs * t_len, d), norm_ffn_g[l], w1, w2, g_out, tile_rows=FFN_ROWS).reshape(bs, t_len, d)
    return (xp, xs) + tuple(jnp.stack(v) for v in outs_p) + tuple(jnp.stack(v) for v in outs_s)
```

```python
import functools

import jax
import jax.numpy as jnp
from jax import lax
from jax.experimental import pallas as pl
from jax.experimental.pallas import tpu as pltpu

F32 = jnp.float32
BF16 = jnp.bfloat16

CHUNK = 64
A_HEADS = 8
A_KV_HEADS = 2
HEAD_DIM = 64
WINDOW = 128
ROPE_DIM = 16
ROPE_THETA = 500000.0
CONV_W = 31
M_HEADS = 4
QK_CONV_W = 4
PAST_LEN = 4096
EPS = 1e-6

LANES = 128
SUBLANES = 8
VMEM_LIMIT_BYTES = 56 * 1024 * 1024

NEG_INF = float("-inf")


def _rms(x, g):
    return x * lax.rsqrt(jnp.mean(x * x, axis=-1, keepdims=True) + EPS) * g


def _sigmoid(x):
    return 0.5 * jnp.tanh(0.5 * x) + 0.5


def _log_sigmoid(x):
    return jnp.minimum(x, 0.0) - jnp.log(1.0 + jnp.exp(-jnp.abs(x)))


def _dot(a, b):
    return jnp.dot(a, b, preferred_element_type=F32)


def _dot_nt(a, b):
    return lax.dot_general(a, b, (((1,), (1,)), ((), ())), preferred_element_type=F32)


def _dot_tn(a, b):
    return lax.dot_general(a, b, (((0,), (0,)), ((), ())), preferred_element_type=F32)


SINGLE_BUFFER_MIN_ELEMS = 1 << 20


def _const_spec(shape):
    nd = len(shape)
    n_elems = 1
    for s in shape:
        n_elems *= s
    if n_elems >= SINGLE_BUFFER_MIN_ELEMS:
        return pl.BlockSpec(shape, lambda *_: (0,) * nd, pipeline_mode=pl.Buffered(1))
    return pl.BlockSpec(shape, lambda *_: (0,) * nd)


def _ffn_body(*refs, n_hidden_chunks, hidden_chunk, final_norm):
    if final_norm:
        x_ref, g_ref, w1_ref, w2_ref, gout_ref, o_ref = refs
    else:
        x_ref, g_ref, w1_ref, w2_ref, o_ref = refs
    x = x_ref[...]
    xb = _rms(x, g_ref[...]).astype(BF16)
    acc = x
    for c in range(n_hidden_chunks):
        cols = slice(c * hidden_chunk, (c + 1) * hidden_chunk)
        h = jnp.maximum(_dot(xb, w1_ref[:, cols]), 0.0)
        acc = acc + _dot((h * h).astype(BF16), w2_ref[cols, :])
    if final_norm:
        acc = _rms(acc, gout_ref[...])
    o_ref[...] = acc


def _ffn(x, g, w1, w2, g_out, *, tile_rows):
    n, d = x.shape
    d_ff = w1.shape[1]
    hidden_chunk = min(d_ff, 1024)
    final_norm = g_out is not None
    tile_rows = min(tile_rows, n)
    assert n % tile_rows == 0 and d_ff % hidden_chunk == 0
    body = functools.partial(_ffn_body, n_hidden_chunks=d_ff // hidden_chunk, hidden_chunk=hidden_chunk,
                             final_norm=final_norm)
    in_specs = [pl.BlockSpec((tile_rows, d), lambda i: (i, 0)), _const_spec((1, d)), _const_spec((d, d_ff)),
                _const_spec((d_ff, d))]
    args = [x, g.reshape(1, d), w1, w2]
    if final_norm:
        in_specs.append(_const_spec((1, d)))
        args.append(g_out.reshape(1, d))
    return pl.pallas_call(
        body,
        grid=(n // tile_rows,),
        in_specs=in_specs,
        out_specs=pl.BlockSpec((tile_rows, d), lambda i: (i, 0)),
        out_shape=jax.ShapeDtypeStruct((n, d), F32),
        compiler_params=pltpu.CompilerParams(dimension_semantics=("arbitrary",), vmem_limit_bytes=VMEM_LIMIT_BYTES),
        name="ffn_final" if final_norm else "ffn",
    )(*args)


CONV_PAD = 32
CONV_ROWS = 64


def _even_body(*refs, tile, q_chunk, has_hist, n_tiles):
    if has_hist:
        (x_ref, g_ref, win_ref, cos_ref, sa_ref, sb_ref, sink_ref, cw_ref, cb_ref, lng_ref, lnb_ref, wout_ref,
         kh_ref, vh_ref, ch_ref, xo_ref, ko_ref, vo_ref, co_ref,
         z_ref, q_ref, kext_ref, vext_ref, kvar_ref, vvar_ref, uext_ref, mix_ref, s_ref, p_ref) = refs
    else:
        (x_ref, g_ref, win_ref, cos_ref, sa_ref, sb_ref, sink_ref, cw_ref, cb_ref, lng_ref, lnb_ref, wout_ref,
         xo_ref, ko_ref, vo_ref, co_ref,
         z_ref, q_ref, kext_ref, vext_ref, kvar_ref, vvar_ref, uext_ref, mix_ref, s_ref, p_ref) = refs
    t = pl.program_id(1)
    a_q = A_HEADS * HEAD_DIM
    a_kv = A_KV_HEADS * HEAD_DIM
    conv_ch = cw_ref.shape[1]
    n_slab = conv_ch // LANES
    n_keys = WINDOW + q_chunk
    hist_rows = CONV_W - 1

    @pl.when(t == 0)
    def _init():
        for j in range(n_slab):
            uext_ref[j, 0:CONV_PAD, :] = jnp.zeros((CONV_PAD, LANES), F32)
        if has_hist:
            kext_ref[0:WINDOW, :] = kh_ref[0]
            vext_ref[0:WINDOW, :] = vh_ref[0]
            for j in range(n_slab):
                uext_ref[j, CONV_PAD - hist_rows:CONV_PAD, :] = ch_ref[0, :, j * LANES:(j + 1) * LANES]
        else:
            kext_ref[0:WINDOW, :] = jnp.zeros((WINDOW, a_kv), F32)
            vext_ref[0:WINDOW, :] = jnp.zeros((WINDOW, a_kv), F32)

    x = x_ref[0]
    hn = _rms(x, g_ref[...]).astype(BF16)
    z_ref[...] = _dot(hn, win_ref[...])

    cos_t = cos_ref[...]
    sin_a = sa_ref[...]
    sin_b = sb_ref[...]
    half = ROPE_DIM // 2

    def rope(v):
        return v * cos_t + pltpu.roll(v, LANES - half, 1) * sin_a + pltpu.roll(v, half, 1) * sin_b

    scale = HEAD_DIM ** -0.5
    for j in range(a_q // LANES):
        cols = slice(j * LANES, (j + 1) * LANES)
        q_ref[:, cols] = (rope(z_ref[:, cols]) * scale).astype(BF16)
    kext_ref[WINDOW:WINDOW + tile, :] = rope(z_ref[:, a_q:a_q + a_kv])
    vext_ref[WINDOW:WINDOW + tile, :] = z_ref[:, a_q + a_kv:a_q + 2 * a_kv]

    lane = lax.broadcasted_iota(jnp.int32, (1, LANES), 1)
    low = lane < HEAD_DIM
    for src_ref, var_ref in ((kext_ref, kvar_ref), (vext_ref, vvar_ref)):
        kx = src_ref[...]
        kr = pltpu.roll(kx, HEAD_DIM, 1)
        var_ref[0] = jnp.where(low, kx, 0.0).astype(BF16)
        var_ref[1] = jnp.where(low, 0.0, kr).astype(BF16)
        var_ref[2] = jnp.where(low, kr, 0.0).astype(BF16)
        var_ref[3] = jnp.where(low, 0.0, kx).astype(BF16)

    group = A_HEADS // A_KV_HEADS
    row = lax.broadcasted_iota(jnp.int32, (2 * q_chunk, 1), 0)
    first_rows = row < q_chunk

    n_chunks = tile // q_chunk
    for c in range(n_chunks):
        r0 = c * q_chunk
        for g in range(A_KV_HEADS):
            qa = q_ref[r0:r0 + q_chunk, 2 * g * LANES:(2 * g + 1) * LANES]
            qb = q_ref[r0:r0 + q_chunk, (2 * g + 1) * LANES:(2 * g + 2) * LANES]
            qs = jnp.concatenate([qa, qb], axis=0)
            for hi in range(2):
                s_ref[(c * A_KV_HEADS + g) * 2 + hi] = _dot_nt(qs, kvar_ref[2 * g + hi, r0:r0 + n_keys, :])
    for c in range(n_chunks):
        r0 = c * q_chunk
        masked = (not has_hist) and r0 < WINDOW
        if masked:
            key_pos = t * tile + r0 - WINDOW + lax.broadcasted_iota(jnp.int32, (1, n_keys), 1)
            bias = jnp.where(key_pos >= 0, 0.0, NEG_INF)
        for g in range(A_KV_HEADS):
            for hi in range(2):
                i = (c * A_KV_HEADS + g) * 2 + hi
                s = s_ref[i]
                if masked:
                    s = s + bias
                sk = jnp.where(first_rows, sink_ref[group * g + hi], sink_ref[group * g + 2 + hi])
                mx = jnp.maximum(jnp.max(s, axis=-1, keepdims=True), sk)
                p = jnp.exp(s - mx)
                den = jnp.sum(p, axis=-1, keepdims=True) + jnp.exp(sk - mx)
                p_ref[i] = (p * (1.0 / den)).astype(BF16)
    for c in range(n_chunks):
        r0 = c * q_chunk
        for g in range(A_KV_HEADS):
            i = (c * A_KV_HEADS + g) * 2
            acc = (_dot(p_ref[i], vvar_ref[2 * g, r0:r0 + n_keys, :])
                   + _dot(p_ref[i + 1], vvar_ref[2 * g + 1, r0:r0 + n_keys, :]))
            mix_ref[r0:r0 + q_chunk, 2 * g * LANES:(2 * g + 1) * LANES] = acc[0:q_chunk].astype(BF16)
            mix_ref[r0:r0 + q_chunk, (2 * g + 1) * LANES:(2 * g + 2) * LANES] = acc[q_chunk:].astype(BF16)

    glu0 = a_q + 2 * a_kv
    for j in range(n_slab):
        lo = glu0 + j * LANES
        uext_ref[j, CONV_PAD:CONV_PAD + tile, :] = (
            z_ref[:, lo:lo + LANES] * _sigmoid(z_ref[:, lo + conv_ch:lo + conv_ch + LANES]))
    conv_rows = min(CONV_ROWS, tile)
    for i in range(tile // conv_rows):
        r0 = i * conv_rows
        accs = []
        for j in range(n_slab):
            cols = slice(j * LANES, (j + 1) * LANES)
            acc = jnp.zeros((conv_rows, LANES), F32)
            for w in range(CONV_W):
                acc = acc + uext_ref[j, pl.ds(r0 + CONV_PAD - hist_rows + w, conv_rows), :] * cw_ref[w:w + 1, cols]
            accs.append(acc + cb_ref[:, cols])
        mean = sum(jnp.sum(a, axis=-1, keepdims=True) for a in accs) * (1.0 / conv_ch)
        cent = [a - mean for a in accs]
        var = sum(jnp.sum(xc * xc, axis=-1, keepdims=True) for xc in cent) * (1.0 / conv_ch)
        rstd = lax.rsqrt(var + EPS)
        for j in range(n_slab):
            cols = slice(j * LANES, (j + 1) * LANES)
            y = cent[j] * rstd * lng_ref[:, cols] + lnb_ref[:, cols]
            mix_ref[r0:r0 + conv_rows, a_q + j * LANES:a_q + (j + 1) * LANES] = (y * _sigmoid(y)).astype(BF16)

    xo_ref[0] = x + _dot(mix_ref[...], wout_ref[...])

    ko_ref[0] = kext_ref[tile:tile + WINDOW, :]
    vo_ref[0] = vext_ref[tile:tile + WINDOW, :]
    for j in range(n_slab):
        co_ref[0, :, j * LANES:(j + 1) * LANES] = uext_ref[j, tile + CONV_PAD - hist_rows:tile + CONV_PAD, :]
    if n_tiles > 1:
        kext_ref[0:WINDOW, :] = kext_ref[tile:tile + WINDOW, :]
        vext_ref[0:WINDOW, :] = vext_ref[tile:tile + WINDOW, :]
        for j in range(n_slab):
            uext_ref[j, 0:CONV_PAD, :] = uext_ref[j, tile:tile + CONV_PAD, :]


def _rope_tables(pos0, t_len):
    half = ROPE_DIM // 2
    pos = pos0 + jnp.arange(t_len, dtype=jnp.int32)
    inv = ROPE_THETA ** (-jnp.arange(half, dtype=F32) / half)
    ang = pos.astype(F32)[:, None] * inv[None, :]
    cos = jnp.cos(ang)
    sin = jnp.sin(ang)
    ones = jnp.ones((t_len, HEAD_DIM - ROPE_DIM), F32)
    zeros = jnp.zeros((t_len, HEAD_DIM - ROPE_DIM), F32)
    zh = jnp.zeros((t_len, half), F32)
    reps = LANES // HEAD_DIM
    cos_t = jnp.tile(jnp.concatenate([cos, cos, ones], axis=1), (1, reps))
    sin_a = jnp.tile(jnp.concatenate([-sin, zh, zeros], axis=1), (1, reps))
    sin_b = jnp.tile(jnp.concatenate([zh, sin, zeros], axis=1), (1, reps))
    return cos_t, sin_a, sin_b


def _even_layer(x, g, w_in, sink, cw, cb, lng, lnb, w_out, hist, pos0, *, tile, q_chunk):
    bsz, t_len, d = x.shape
    tile = min(tile, t_len)
    assert t_len % tile == 0 and tile % q_chunk == 0
    n_tiles = t_len // tile
    has_hist = hist is not None
    assert has_hist or q_chunk == CHUNK
    assert (not has_hist) or n_tiles == 1
    e_in = w_in.shape[1]
    conv_ch = cw.shape[1]
    a_kv = A_KV_HEADS * HEAD_DIM
    a_q = A_HEADS * HEAD_DIM
    hist_rows = CONV_W - 1
    cos_t, sin_a, sin_b = _rope_tables(pos0, t_len)

    tab_spec = pl.BlockSpec((tile, LANES), lambda b, t: (t, 0))
    in_specs = [
        pl.BlockSpec((1, tile, d), lambda b, t: (b, t, 0)),
        _const_spec((1, d)), _const_spec((d, e_in)),
        tab_spec, tab_spec, tab_spec,
        pl.BlockSpec(memory_space=pltpu.SMEM),
        _const_spec((CONV_W, conv_ch)), _const_spec((1, conv_ch)), _const_spec((1, conv_ch)),
        _const_spec((1, conv_ch)), _const_spec((a_q + conv_ch, d)),
    ]
    args = [x, g.reshape(1, d), w_in, cos_t, sin_a, sin_b, sink, cw, cb.reshape(1, conv_ch),
            lng.reshape(1, conv_ch), lnb.reshape(1, conv_ch), w_out]
    if has_hist:
        layer, k_hist, v_hist, c_hist = hist
        in_specs += [pl.BlockSpec((1, WINDOW, a_kv), lambda b, t: (b, 0, 0)),
                     pl.BlockSpec((1, WINDOW, a_kv), lambda b, t: (b, 0, 0)),
                     pl.BlockSpec((None, 1, hist_rows, conv_ch), lambda b, t: (layer, b, 0, 0))]
        args += [k_hist[layer].reshape(bsz, WINDOW, a_kv), v_hist[layer].reshape(bsz, WINDOW, a_kv), c_hist]
    out_specs = [pl.BlockSpec((1, tile, d), lambda b, t: (b, t, 0)),
                 pl.BlockSpec((1, WINDOW, a_kv), lambda b, t: (b, 0, 0)),
                 pl.BlockSpec((1, WINDOW, a_kv), lambda b, t: (b, 0, 0)),
                 pl.BlockSpec((1, hist_rows, conv_ch), lambda b, t: (b, 0, 0))]
    out_shape = [jax.ShapeDtypeStruct((bsz, t_len, d), F32),
                 jax.ShapeDtypeStruct((bsz, WINDOW, a_kv), F32),
                 jax.ShapeDtypeStruct((bsz, WINDOW, a_kv), F32),
                 jax.ShapeDtypeStruct((bsz, hist_rows, conv_ch), F32)]
    n_blocks = 2 * A_KV_HEADS * (tile // q_chunk)
    scratch = [pltpu.VMEM((tile, e_in), F32),
               pltpu.VMEM((tile, a_q), BF16),
               pltpu.VMEM((WINDOW + tile, a_kv), F32),
               pltpu.VMEM((WINDOW + tile, a_kv), F32),
               pltpu.VMEM((4, WINDOW + tile, LANES), BF16),
               pltpu.VMEM((4, WINDOW + tile, LANES), BF16),
               pltpu.VMEM((conv_ch // LANES, CONV_PAD + tile, LANES), F32),
               pltpu.VMEM((tile, a_q + conv_ch), BF16),
               pltpu.VMEM((n_blocks, 2 * q_chunk, WINDOW + q_chunk), F32),
               pltpu.VMEM((n_blocks, 2 * q_chunk, WINDOW + q_chunk), BF16)]
    body = functools.partial(_even_body, tile=tile, q_chunk=q_chunk, has_hist=has_hist, n_tiles=n_tiles)
    xo, ko, vo, co = pl.pallas_call(
        body,
        grid=(bsz, n_tiles),
        in_specs=in_specs,
        out_specs=out_specs,
        out_shape=out_shape,
        scratch_shapes=scratch,
        compiler_params=pltpu.CompilerParams(dimension_semantics=("arbitrary", "arbitrary"),
                                             vmem_limit_bytes=VMEM_LIMIT_BYTES),
        name="even_mixer_hist" if has_hist else "even_mixer",
    )(*args)
    kv_shape = (bsz, WINDOW, A_KV_HEADS, HEAD_DIM)
    return xo, ko.reshape(kv_shape), vo.reshape(kv_shape), co


QK_PAD = 8
QK_CONV_ROWS = 128
GATE_LANE0 = M_HEADS


def _cumsum_rows(x):
    n = x.shape[0]
    tri = (lax.broadcasted_iota(jnp.int32, (n, n), 0) >= lax.broadcasted_iota(jnp.int32, (n, n), 1)).astype(BF16)
    x1 = x.astype(BF16)
    r1 = x - x1.astype(F32)
    x2 = r1.astype(BF16)
    x3 = (r1 - x2.astype(F32)).astype(BF16)
    return _dot(tri, x1) + _dot(tri, x2) + _dot(tri, x3)


def _odd_body(*refs, tile, m_chunk, has_hist, n_tiles):
    if has_hist:
        (x_ref, g_ref, win_ref, wg_ref, gb_ref, cw_ref, cb_ref, ng_ref, wout_ref,
         qkh_ref, c0_ref, n0_ref, m0_ref, xo_ref, qko_ref, co_ref, no_ref, mo_ref,
         qkext_ref, z_ref, qs_ref, ks_ref, mix_ref, c_ref, n_ref, m_ref,
         gq_ref, rt_ref, sqk_ref, sbf_ref, rs_ref, wi_ref, em_ref, vb_ref, vw_ref, dn_ref, num_ref, dc_ref, cst_ref, nst_ref,
         numc_ref) = refs
    else:
        (x_ref, g_ref, win_ref, wg_ref, gb_ref, cw_ref, cb_ref, ng_ref, wout_ref,
         xo_ref, qko_ref, co_ref, no_ref, mo_ref,
         qkext_ref, z_ref, qs_ref, ks_ref, mix_ref, c_ref, n_ref, m_ref,
         gq_ref, rt_ref, sqk_ref, sbf_ref, rs_ref, wi_ref, em_ref, vb_ref, vw_ref, dn_ref, num_ref, dc_ref, cst_ref, nst_ref,
         numc_ref) = refs
    t = pl.program_id(1)
    width = ng_ref.shape[1]
    m_dim = width // M_HEADS
    n_slab = 2 * width // LANES
    hist_rows = QK_CONV_W - 1
    length = m_chunk

    @pl.when(t == 0)
    def _init():
        for j in range(n_slab):
            qkext_ref[j, 0:QK_PAD, :] = jnp.zeros((QK_PAD, LANES), F32)
        if has_hist:
            for j in range(n_slab):
                qkext_ref[j, QK_PAD - hist_rows:QK_PAD, :] = qkh_ref[0, :, j * LANES:(j + 1) * LANES]
            c_ref[...] = c0_ref[0]
            n_ref[...] = n0_ref[0]
            m_ref[...] = m0_ref[0]
        else:
            c_ref[...] = jnp.zeros(c_ref.shape, F32)
            n_ref[...] = jnp.zeros(n_ref.shape, F32)
            m_ref[...] = jnp.zeros(m_ref.shape, F32)

    x = x_ref[0]
    hn = _rms(x, g_ref[...]).astype(BF16)
    for n in range(n_slab // 2):
        r = _dot(hn, win_ref[:, 2 * n * LANES:(2 * n + 2) * LANES])
        qkext_ref[2 * n, QK_PAD:QK_PAD + tile, :] = r[:, 0:LANES]
        qkext_ref[2 * n + 1, QK_PAD:QK_PAD + tile, :] = r[:, LANES:2 * LANES]
    z_ref[...] = _dot(hn, win_ref[:, 2 * width:4 * width])
    gts = _dot(hn, wg_ref[...]) + gb_ref[...]

    conv_rows = min(QK_CONV_ROWS, tile)
    k_scale = m_dim ** -0.5
    for j in range(n_slab):
        cols = slice(j * LANES, (j + 1) * LANES)
        is_q = j * LANES < width
        dst_ref = qs_ref if is_q else ks_ref
        dcols = slice(j * LANES % width, j * LANES % width + LANES)
        for i in range(tile // conv_rows):
            r0 = i * conv_rows
            acc = jnp.zeros((conv_rows, LANES), F32)
            for w in range(QK_CONV_W):
                acc = acc + qkext_ref[j, pl.ds(r0 + QK_PAD - hist_rows + w, conv_rows), :] * cw_ref[w:w + 1, cols]
            acc = acc + cb_ref[:, cols]
            y = acc * _sigmoid(acc)
            if not is_q:
                y = y * k_scale
            dst_ref[r0:r0 + conv_rows, dcols] = y.astype(BF16)

    lane = lax.broadcasted_iota(jnp.int32, (1, LANES), 1)
    gate_lanes = (lane >= GATE_LANE0) & (lane < GATE_LANE0 + M_HEADS)
    lf_all = jnp.where(gate_lanes, _log_sigmoid(gts), 0.0)
    ig_all = jnp.where(gate_lanes, pltpu.roll(gts, GATE_LANE0, 1), 0.0)
    causal = (lax.broadcasted_iota(jnp.int32, (length, length), 0)
              >= lax.broadcasted_iota(jnp.int32, (length, length), 1))
    pad_rows = LANES - length if length < LANES else 0
    n_ch = tile // length
    lrows = slice(0, length)


    m_run = m_ref[...]
    w_c_rows = []
    for c in range(n_ch):
        rows = slice(c * length, (c + 1) * length)
        bcum = _cumsum_rows(lf_all[rows])
        b_end = bcum[length - 1:length, :]
        r_all = ig_all[rows] - bcum
        dec = b_end + r_all
        m_new = jnp.maximum(b_end + m_run, jnp.max(dec, axis=0, keepdims=True))
        gq_ref[c, 0, lrows, :] = bcum
        gq_ref[c, 1, lrows, :] = bcum + m_run
        gq_ref[c, 2, lrows, :] = jnp.exp(dec - m_new)
        r_pad = r_all if pad_rows == 0 else jnp.concatenate([r_all, jnp.zeros((pad_rows, LANES), F32)], axis=0)
        rt_ref[c] = r_pad.T
        w_c_rows.append(jnp.exp(b_end + m_run - m_new))
        m_run = jnp.where(gate_lanes, m_new, 0.0)
    m_ref[...] = m_run

    def pieces():
        for c in range(n_ch):
            for h in range(M_HEADS):
                yield (c, h, c * M_HEADS + h, slice(c * length, (c + 1) * length), slice(h * m_dim, (h + 1) * m_dim),
                       GATE_LANE0 + h)

    for c, h, i, rows, hcols, gl in pieces():
        sqk_ref[i] = _dot_nt(qs_ref[rows, hcols], ks_ref[rows, hcols])

    for c, h, i, rows, hcols, gl in pieces():
        a_h = gq_ref[c, 1, lrows, gl:gl + 1]
        dmat = jnp.where(causal, gq_ref[c, 0, lrows, gl:gl + 1] + rt_ref[c, gl:gl + 1, lrows], NEG_INF)
        m_t = jnp.maximum(a_h, jnp.max(dmat, axis=-1, keepdims=True))
        m_rep = jnp.broadcast_to(m_t, (length, LANES))
        a_rep = jnp.broadcast_to(a_h, (length, LANES))
        m_wide = jnp.tile(m_rep, (1, length // LANES)) if length % LANES == 0 else m_t
        s = sqk_ref[i] * jnp.exp(dmat - m_wide)
        sbf_ref[i] = s.astype(BF16)
        rs_ref[i] = jnp.broadcast_to(jnp.sum(s, axis=-1, keepdims=True), (length, LANES))
        wi_ref[i] = jnp.exp(a_rep - m_rep)
        em_ref[i] = jnp.exp(-m_rep)
        w_s = gq_ref[c, 2, lrows, gl:gl + 1]
        vh = z_ref[rows, hcols]
        vb_ref[rows, hcols] = vh.astype(BF16)
        vw_ref[rows, hcols] = (vh * w_s).astype(BF16)
        dn_ref[i:i + 1, :] = jnp.sum(ks_ref[rows, hcols].astype(F32) * w_s, axis=0, keepdims=True)

    for c, h, i, rows, hcols, gl in pieces():
        num_ref[rows, hcols] = _dot(sbf_ref[i], vb_ref[rows, hcols])
        dc_ref[i] = _dot_tn(vw_ref[rows, hcols], ks_ref[rows, hcols])

    for c, h, i, rows, hcols, gl in pieces():
        w_c = w_c_rows[c][:, gl:gl + 1]
        c_h = c_ref[h]
        cst_ref[i] = c_h.astype(BF16)
        c_ref[h] = w_c * c_h + dc_ref[i]
        n_h = n_ref[h:h + 1, :]
        nst_ref[i:i + 1, :] = n_h
        n_ref[h:h + 1, :] = w_c * n_h + dn_ref[i:i + 1, :]

    for c, h, i, rows, hcols, gl in pieces():
        numc_ref[rows, hcols] = _dot_nt(qs_ref[rows, hcols], cst_ref[i])

    half_cols = m_dim // LANES
    for c, h, i, rows, hcols, gl in pieces():
        w_inter = wi_ref[i]
        qn = jnp.sum(qs_ref[rows, hcols].astype(F32) * nst_ref[i:i + 1, :], axis=-1, keepdims=True)
        den = rs_ref[i] + w_inter * qn
        inv = 1.0 / jnp.maximum(jnp.abs(den), em_ref[i])
        hh = []
        for j in range(half_cols):
            cols = slice(h * m_dim + j * LANES, h * m_dim + (j + 1) * LANES)
            hh.append((num_ref[rows, cols] + w_inter * numc_ref[rows, cols]) * inv)
        ms = sum(jnp.sum(v * v, axis=-1, keepdims=True) for v in hh) * (1.0 / m_dim)
        rstd = lax.rsqrt(ms + EPS)
        for j in range(half_cols):
            cols = slice(h * m_dim + j * LANES, h * m_dim + (j + 1) * LANES)
            og = _sigmoid(z_ref[rows, width + h * m_dim + j * LANES:width + h * m_dim + (j + 1) * LANES])
            mix_ref[rows, cols] = (hh[j] * rstd * ng_ref[:, cols] * og).astype(BF16)

    xo_ref[0] = x + _dot(mix_ref[...], wout_ref[...])

    for j in range(n_slab):
        qko_ref[0, :, j * LANES:(j + 1) * LANES] = qkext_ref[j, tile + QK_PAD - hist_rows:tile + QK_PAD, :]
    co_ref[0] = c_ref[...]
    no_ref[0] = n_ref[...]
    mo_ref[0] = m_ref[...]
    if n_tiles > 1:
        for j in range(n_slab):
            qkext_ref[j, 0:QK_PAD, :] = qkext_ref[j, tile:tile + QK_PAD, :]


def _odd_layer(x, g, w_in, w_gate, gate_bias, cw, cb, ng, w_out, hist, *, tile, m_chunk):
    bsz, t_len, d = x.shape
    tile = min(tile, t_len)
    m_chunk = min(m_chunk, tile)
    assert t_len % tile == 0 and tile % m_chunk == 0
    n_tiles = t_len // tile
    has_hist = hist is not None
    assert (not has_hist) or n_tiles == 1
    width = ng.shape[0]
    m_dim = width // M_HEADS
    hist_rows = QK_CONV_W - 1
    n_ch = tile // m_chunk
    n_hc = n_ch * M_HEADS

    in_specs = [
        pl.BlockSpec((1, tile, d), lambda b, t: (b, t, 0)),
        _const_spec((1, d)), _const_spec((d, 4 * width)), _const_spec((d, LANES)), _const_spec((1, LANES)),
        _const_spec((QK_CONV_W, 2 * width)), _const_spec((1, 2 * width)), _const_spec((1, width)),
        _const_spec((width, d)),
    ]
    args = [x, g.reshape(1, d), w_in, w_gate, gate_bias, cw, cb.reshape(1, 2 * width), ng.reshape(1, width), w_out]
    if has_hist:
        layer, qk_hist, c0, n0, m0 = hist
        m0_lanes = jnp.pad(m0[layer], ((0, 0), (GATE_LANE0, LANES - GATE_LANE0 - M_HEADS))).reshape(bsz, 1, LANES)
        in_specs += [pl.BlockSpec((None, 1, hist_rows, 2 * width), lambda b, t: (layer, b, 0, 0)),
                     pl.BlockSpec((None, 1, M_HEADS, m_dim, m_dim), lambda b, t: (layer, b, 0, 0, 0)),
                     pl.BlockSpec((None, 1, M_HEADS, m_dim), lambda b, t: (layer, b, 0, 0)),
                     pl.BlockSpec((1, 1, LANES), lambda b, t: (b, 0, 0))]
        args += [qk_hist, c0, n0, m0_lanes]
    out_specs = [pl.BlockSpec((1, tile, d), lambda b, t: (b, t, 0)),
                 pl.BlockSpec((1, hist_rows, 2 * width), lambda b, t: (b, 0, 0)),
                 pl.BlockSpec((1, M_HEADS, m_dim, m_dim), lambda b, t: (b, 0, 0, 0)),
                 pl.BlockSpec((1, M_HEADS, m_dim), lambda b, t: (b, 0, 0)),
                 pl.BlockSpec((1, 1, LANES), lambda b, t: (b, 0, 0))]
    out_shape = [jax.ShapeDtypeStruct((bsz, t_len, d), F32),
                 jax.ShapeDtypeStruct((bsz, hist_rows, 2 * width), F32),
                 jax.ShapeDtypeStruct((bsz, M_HEADS, m_dim, m_dim), F32),
                 jax.ShapeDtypeStruct((bsz, M_HEADS, m_dim), F32),
                 jax.ShapeDtypeStruct((bsz, 1, LANES), F32)]
    scratch = [pltpu.VMEM((2 * width // LANES, QK_PAD + tile, LANES), F32),
               pltpu.VMEM((tile, 2 * width), F32),
               pltpu.VMEM((tile, width), BF16),
               pltpu.VMEM((tile, width), BF16),
               pltpu.VMEM((tile, width), BF16),
               pltpu.VMEM((M_HEADS, m_dim, m_dim), F32),
               pltpu.VMEM((M_HEADS, m_dim), F32),
               pltpu.VMEM((1, LANES), F32),
               pltpu.VMEM((n_ch, 3, max(m_chunk, LANES), LANES), F32),
               pltpu.VMEM((n_ch, LANES, max(m_chunk, LANES)), F32),
               pltpu.VMEM((n_hc, m_chunk, m_chunk), F32),
               pltpu.VMEM((n_hc, m_chunk, m_chunk), BF16),
               pltpu.VMEM((n_hc, m_chunk, LANES), F32),
               pltpu.VMEM((n_hc, m_chunk, LANES), F32),
               pltpu.VMEM((n_hc, m_chunk, LANES), F32),
               pltpu.VMEM((tile, width), BF16),
               pltpu.VMEM((tile, width), BF16),
               pltpu.VMEM((n_hc, m_dim), F32),
               pltpu.VMEM((tile, width), F32),
               pltpu.VMEM((n_hc, m_dim, m_dim), F32),
               pltpu.VMEM((n_hc, m_dim, m_dim), BF16),
               pltpu.VMEM((n_hc, m_dim), F32),
               pltpu.VMEM((tile, width), F32)]
    body = functools.partial(_odd_body, tile=tile, m_chunk=m_chunk, has_hist=has_hist, n_tiles=n_tiles)
    xo, qko, co, no, mo = pl.pallas_call(
        body,
        grid=(bsz, n_tiles),
        in_specs=in_specs,
        out_specs=out_specs,
        out_shape=out_shape,
        scratch_shapes=scratch,
        compiler_params=pltpu.CompilerParams(dimension_semantics=("arbitrary", "arbitrary"),
                                             vmem_limit_bytes=VMEM_LIMIT_BYTES),
        name="mlstm_mixer_hist" if has_hist else "mlstm_mixer",
    )(*args)
    return xo, qko, co, no, mo[:, 0, GATE_LANE0:GATE_LANE0 + M_HEADS]


EVEN_TILE = 1024
ODD_TILE = 512
MLSTM_CHUNK = 128
FFN_ROWS = 1024


def kernel(x_prompt, x_sample, cache_swa_k, cache_swa_v, state_conv, state_qk_conv, state_mlstm_C, state_mlstm_n,
           state_mlstm_m, norm_mix_g, norm_ffn_g, norm_out_g, w_in_even, attn_sink, conv_w, conv_b, conv_ln_g,
           conv_ln_b, w_out_even, w_in_odd, qk_conv_w, qk_conv_b, gate_b_i, gate_b_f, mlstm_norm_g, w_out_odd,
           w_ff1, w_ff2):
    depth = norm_mix_g.shape[0]
    bp, s_len, d = x_prompt.shape
    bs, t_len, _ = x_sample.shape
    width = mlstm_norm_g.shape[1]
    xp, xs = x_prompt, x_sample
    outs_p = [[] for _ in range(7)]
    outs_s = [[] for _ in range(7)]
    for l in range(depth):
        if l % 2 == 0:
            i = l // 2
            prm = (norm_mix_g[l], w_in_even[i].astype(BF16), attn_sink[i], conv_w[i], conv_b[i], conv_ln_g[i],
                   conv_ln_b[i], w_out_even[i].astype(BF16))
            xp, kp, vp, cvp = _even_layer(xp, *prm, None, 0, tile=EVEN_TILE, q_chunk=CHUNK)
            xs, ks_, vs_, cvs = _even_layer(xs, *prm, (i, cache_swa_k, cache_swa_v, state_conv), PAST_LEN,
                                            tile=t_len, q_chunk=t_len)
            for lst, val in zip(outs_p[:3], (kp, vp, cvp)):
                lst.append(val)
            for lst, val in zip(outs_s[:3], (ks_, vs_, cvs)):
                lst.append(val)
        else:
            j = l // 2
            w_in = w_in_odd[j]
            n_gate = 2 * M_HEADS
            w_gate = jnp.pad(w_in[:, 4 * width:], ((0, 0), (0, LANES - n_gate))).astype(BF16)
            gate_bias = jnp.pad(jnp.concatenate([gate_b_i[j], gate_b_f[j]]), (0, LANES - n_gate)).reshape(1, LANES)
            prm = (norm_mix_g[l], w_in[:, :4 * width].astype(BF16), w_gate, gate_bias, qk_conv_w[j], qk_conv_b[j],
                   mlstm_norm_g[j], w_out_odd[j].astype(BF16))
            xp, qkp, cpn, npn, mpn = _odd_layer(xp, *prm, None, tile=ODD_TILE, m_chunk=MLSTM_CHUNK)
            xs, qks, csn, nsn, msn = _odd_layer(
                xs, *prm, (j, state_qk_conv, state_mlstm_C, state_mlstm_n, state_mlstm_m),
                tile=t_len, m_chunk=t_len)
            for lst, val in zip(outs_p[3:], (qkp, cpn, npn, mpn)):
                lst.append(val)
            for lst, val in zip(outs_s[3:], (qks, csn, nsn, msn)):
                lst.append(val)
        g_out = norm_out_g if l == depth - 1 else None
        w1 = w_ff1[l].astype(BF16)
        w2 = w_ff2[l].astype(BF16)
        xp = _ffn(xp.reshape(bp * s_len, d), norm_ffn_g[l], w1, w2, g_out, tile_rows=FFN_ROWS).reshape(bp, s_len, d)
        xs = _ffn(xs.reshape(bs * t_len, d), norm_ffn_g[l], w1, w2, g_out, tile_rows=FFN_ROWS).reshape(bs, t_len, d)
    return (xp, xs) + tuple(jnp.stack(v) for v in outs_p) + tuple(jnp.stack(v) for v in outs_s)
```
